```python
import math
import jax, jax.numpy as jnp
from jax import lax
import numpy as np

D_MODEL = 2048
BATCH = 16
SEQ = 2048
DEPTH = 1
DEC_BATCH = 4
DEC_SEQ = 8192
PAST_LEN = 128

HEAD_DIM = 64
N_HEADS = 16
N_KV_HEADS = 4
ATTN_WIDTH = N_HEADS * HEAD_DIM
KV_WIDTH = N_KV_HEADS * HEAD_DIM
HY_CH = D_MODEL - ATTN_WIDTH
HY_ORDER = 2
MIX_WIDTH = ATTN_WIDTH + HY_CH
IN_WIDTH = ATTN_WIDTH + 2 * KV_WIDTH + (HY_ORDER + 1) * HY_CH
WINDOW = 128
WBLK = 128
POS_EMB_DIM = 33
POS_BANDS = (POS_EMB_DIM - 1) // 2
FILTER_WIDTH = 64
FILTER_OUT = 2 * HY_ORDER * HY_CH
DECAY_FAST = 0.3
DECAY_SLOW = 1.5
DECAY_TARGET = 1e-2
N_EXPERTS = 32
TOP_K = 4
D_FF = D_MODEL
SWIGLU_ALPHA = 1.702
SWIGLU_LIMIT = 7.0
MOE_BLK = 256
EPS = 1e-6

kernel_name = 'hymba_window_gqa_hyena_moe_encoder'


def rms_norm(x, g):
    xf = x.astype(jnp.float32)
    y = xf * lax.rsqrt(jnp.mean(xf * xf, axis=-1, keepdims=True) + EPS)
    return (y * g.astype(jnp.float32)).astype(x.dtype)


def alibi_slopes():
    return jnp.exp2(-8.0 * jnp.arange(1, N_HEADS + 1, dtype=jnp.float32) / N_HEADS)


def banded_window_attention(q, k, v, sinks):
    b, l, _, _ = q.shape
    nb = l // WBLK
    grp = N_HEADS // N_KV_HEADS
    qb = q.reshape(b, nb, WBLK, N_KV_HEADS, grp, HEAD_DIM)
    pad = ((0, 0), (WBLK, WBLK), (0, 0), (0, 0))
    kp = jnp.pad(k, pad).reshape(b, nb + 2, WBLK, N_KV_HEADS, HEAD_DIM)
    vp = jnp.pad(v, pad).reshape(b, nb + 2, WBLK, N_KV_HEADS, HEAD_DIM)
    kw = jnp.concatenate([kp[:, :-2], kp[:, 1:-1], kp[:, 2:]], axis=2)
    vw = jnp.concatenate([vp[:, :-2], vp[:, 1:-1], vp[:, 2:]], axis=2)
    s = jnp.einsum('bnqkgd,bnskd->bnkgqs', qb, kw, preferred_element_type=jnp.float32)
    s = s * (HEAD_DIM ** -0.5)
    qi = jnp.arange(WBLK)[:, None]
    sj = jnp.arange(3 * WBLK)[None, :]
    dist = jnp.abs(qi + WBLK - sj)
    kpos = (jnp.arange(nb)[:, None] - 1) * WBLK + jnp.arange(3 * WBLK)[None, :]
    valid = (dist <= WINDOW)[None] & ((kpos >= 0) & (kpos < l))[:, None, :]
    slopes = alibi_slopes().reshape(N_KV_HEADS, grp)
    s = s - slopes[:, :, None, None] * dist.astype(jnp.float32)
    s = jnp.where(valid[None, :, None, None], s, -jnp.inf)
    sink = sinks.astype(jnp.float32).reshape(N_KV_HEADS, grp)[:, :, None, None]
    m = jnp.maximum(jnp.max(s, axis=-1, keepdims=True), sink)
    p = jnp.exp(s - m)
    p = p / (jnp.sum(p, axis=-1, keepdims=True) + jnp.exp(sink - m))
    o = jnp.einsum('bnkgqs,bnskd->bnqkgd', p.astype(v.dtype), vw)
    return o.reshape(b, l, N_HEADS * HEAD_DIM)


def short_conv_centred(u, w, bias):
    up = jnp.pad(u, ((0, 0), (1, 1), (0, 0)))
    return up[:, :-2] * w[0] + up[:, 1:-1] * w[1] + up[:, 2:] * w[2] + bias


def implicit_filters(l, w1, b1, f1, w2, b2, f2, w3, b3, f3, w4):
    t01 = jnp.linspace(0.0, 1.0, l, dtype=jnp.float32)[:, None]
    ang = 2.0 * math.pi * jnp.arange(l, dtype=jnp.float32)[:, None] / l
    bands = jnp.linspace(1e-4, POS_BANDS - 1, POS_BANDS, dtype=jnp.float32)[None, :]
    z = jnp.concatenate([t01, jnp.cos(bands * ang), -jnp.sin(bands * ang)], axis=-1)
    h = jnp.sin(f1 * (z @ w1 + b1))
    h = jnp.sin(f2 * (h @ w2 + b2))
    h = jnp.sin(f3 * (h @ w3 + b3))
    h = (h @ w4).astype(jnp.float32).reshape(l, 2 * HY_ORDER, HY_CH)
    deltas = jnp.abs(jnp.linspace(math.log(DECAY_FAST) / DECAY_TARGET,
                                  math.log(DECAY_SLOW) / DECAY_TARGET, HY_CH, dtype=jnp.float32))
    decay = jnp.exp(-t01 * deltas[None, :])
    return h * decay[:, None, :]


def bidirectional_fft_conv(u, h_fwd, h_bwd, skip):
    b, l, c = u.shape
    kern = jnp.concatenate([h_fwd.at[0].add(h_bwd[0]), jnp.zeros((1, c), jnp.float32),
                            h_bwd[:0:-1]], axis=0)
    kf = jnp.fft.rfft(kern, axis=0)
    uf = jnp.fft.rfft(u.astype(jnp.float32), n=2 * l, axis=1)
    y = jnp.fft.irfft(uf * kf[None], n=2 * l, axis=1)[:, :l]
    return (y + u.astype(jnp.float32) * skip.astype(jnp.float32)).astype(u.dtype)


def moe_clamped_swiglu(x, w_router, b_router, w_gate_up, b_gate_up, w_down, b_down):
    b, l, d = x.shape
    t = b * l
    xt = x.reshape(t, d)
    logits = (xt @ w_router + b_router).astype(jnp.float32)
    top_val, top_idx = lax.top_k(logits, TOP_K)
    gates = jax.nn.softmax(top_val, axis=-1)
    n = t * TOP_K
    flat_e = top_idx.reshape(n)
    flat_tok = jnp.arange(n, dtype=jnp.int32) // TOP_K
    flat_g = gates.reshape(n)
    order = jnp.argsort(flat_e)
    e_sorted = flat_e[order]
    counts = jnp.bincount(flat_e, length=N_EXPERTS)
    padded = (counts + MOE_BLK - 1) // MOE_BLK * MOE_BLK
    start = jnp.cumsum(counts) - counts
    pend = jnp.cumsum(padded)
    pstart = pend - padded
    dest = pstart[e_sorted] + jnp.arange(n, dtype=jnp.int32) - start[e_sorted]
    n_rows = -(-(n + N_EXPERTS * MOE_BLK) // MOE_BLK) * MOE_BLK
    n_blocks = n_rows // MOE_BLK
    row_tok = jnp.zeros((n_rows,), jnp.int32).at[dest].set(flat_tok[order])
    row_gate = jnp.zeros((n_rows,), jnp.float32).at[dest].set(flat_g[order])
    block_expert = jnp.minimum(jnp.searchsorted(pend, jnp.arange(n_blocks) * MOE_BLK, side='right'),
                               N_EXPERTS - 1)

    def expert_block(args):
        tok, g, e = args
        xb = xt[tok]
        gu = xb @ w_gate_up[e] + b_gate_up[e]
        glu = jnp.minimum(gu[:, :D_FF], SWIGLU_LIMIT)
        lin = jnp.clip(gu[:, D_FF:], -SWIGLU_LIMIT, SWIGLU_LIMIT)
        hdn = glu * jax.nn.sigmoid(SWIGLU_ALPHA * glu) * (lin + 1.0)
        out = hdn @ w_down[e] + b_down[e]
        return out.astype(jnp.float32) * g[:, None]

    rows = lax.map(expert_block, (row_tok.reshape(n_blocks, MOE_BLK),
                                  row_gate.reshape(n_blocks, MOE_BLK), block_expert))
    y = jnp.zeros((t, d), jnp.float32).at[row_tok].add(rows.reshape(n_rows, d))
    return y.astype(x.dtype).reshape(b, l, d)


def encoder_layer(x, g_mix, w_in, g_q, g_k, attn_sinks, w_short, b_short,
                  flt_w1, flt_b1, flt_freq1, flt_w2, flt_b2, flt_freq2, flt_w3, flt_b3, flt_freq3,
                  flt_w4, hyena_skip, g_attn_out, g_hyena_out, w_out, g_ffn,
                  w_router, b_router, w_gate_up, b_gate_up, w_down, b_down):
    b, l, _ = x.shape
    h = rms_norm(x, g_mix)
    proj = h @ w_in
    c0 = ATTN_WIDTH
    c1 = c0 + KV_WIDTH
    c2 = c1 + KV_WIDTH
    q = rms_norm(proj[..., :c0].reshape(b, l, N_HEADS, HEAD_DIM), g_q)
    k = rms_norm(proj[..., c0:c1].reshape(b, l, N_KV_HEADS, HEAD_DIM), g_k)
    v = proj[..., c1:c2].reshape(b, l, N_KV_HEADS, HEAD_DIM)
    attn = banded_window_attention(q, k, v, attn_sinks)
    u = short_conv_centred(proj[..., c2:], w_short, b_short)
    z = u[..., :HY_CH]
    filt = implicit_filters(l, flt_w1, flt_b1, flt_freq1, flt_w2, flt_b2, flt_freq2,
                            flt_w3, flt_b3, flt_freq3, flt_w4)
    for o in range(HY_ORDER):
        gate = u[..., (o + 1) * HY_CH:(o + 2) * HY_CH]
        z = gate * bidirectional_fft_conv(z, filt[:, 2 * o], filt[:, 2 * o + 1], hyena_skip[o])
    mixed = jnp.concatenate([rms_norm(attn, g_attn_out), rms_norm(z, g_hyena_out)], axis=-1) @ w_out
    x = x + mixed
    return x + moe_clamped_swiglu(rms_norm(x, g_ffn), w_router, b_router,
                                  w_gate_up, b_gate_up, w_down, b_down)


def setup_inputs(seed: int = 0) -> dict:
    key = jax.random.key(seed)
    ks = jax.random.split(key, 30)

    def nrm(k, shape, scale):
        return jax.random.normal(k, shape, jnp.float32) * scale

    def gain(k, shape):
        return 1.0 + nrm(k, shape, 0.02)

    L_ = DEPTH
    return {
        'x_prompt': nrm(ks[0], (BATCH, SEQ, D_MODEL), 1.0),
        'x_sample': nrm(ks[1], (DEC_BATCH, DEC_SEQ, D_MODEL), 1.0),
        'g_mix': gain(ks[2], (L_, D_MODEL)),
        'w_in': nrm(ks[3], (L_, D_MODEL, IN_WIDTH), D_MODEL ** -0.5),
        'g_q': gain(ks[4], (L_, HEAD_DIM)),
        'g_k': gain(ks[5], (L_, HEAD_DIM)),
        'attn_sinks': nrm(ks[6], (L_, N_HEADS), 0.5),
        'w_short': nrm(ks[7], (L_, 3, (HY_ORDER + 1) * HY_CH), 3 ** -0.5),
        'b_short': nrm(ks[8], (L_, (HY_ORDER + 1) * HY_CH), 0.01),
        'flt_w1': nrm(ks[9], (L_, POS_EMB_DIM, FILTER_WIDTH), POS_EMB_DIM ** -0.5),
        'flt_b1': nrm(ks[10], (L_, FILTER_WIDTH), 0.1),
        'flt_freq1': gain(ks[11], (L_, FILTER_WIDTH)),
        'flt_w2': nrm(ks[12], (L_, FILTER_WIDTH, FILTER_WIDTH), FILTER_WIDTH ** -0.5),
        'flt_b2': nrm(ks[13], (L_, FILTER_WIDTH), 0.1),
        'flt_freq2': gain(ks[14], (L_, FILTER_WIDTH)),
        'flt_w3': nrm(ks[15], (L_, FILTER_WIDTH, FILTER_WIDTH), FILTER_WIDTH ** -0.5),
        'flt_b3': nrm(ks[16], (L_, FILTER_WIDTH), 0.1),
        'flt_freq3': gain(ks[17], (L_, FILTER_WIDTH)),
        'flt_w4': nrm(ks[18], (L_, FILTER_WIDTH, FILTER_OUT), FILTER_WIDTH ** -0.5),
        'hyena_skip': nrm(ks[19], (L_, HY_ORDER, HY_CH), 1.0),
        'g_attn_out': gain(ks[20], (L_, ATTN_WIDTH)),
        'g_hyena_out': gain(ks[21], (L_, HY_CH)),
        'w_out': nrm(ks[22], (L_, MIX_WIDTH, D_MODEL), MIX_WIDTH ** -0.5),
        'g_ffn': gain(ks[23], (L_, D_MODEL)),
        'w_router': nrm(ks[24], (L_, D_MODEL, N_EXPERTS), D_MODEL ** -0.5),
        'b_router': nrm(ks[25], (L_, N_EXPERTS), 0.01),
        'w_gate_up': nrm(ks[26], (L_, N_EXPERTS, D_MODEL, 2 * D_FF), D_MODEL ** -0.5),
        'b_gate_up': nrm(ks[27], (L_, N_EXPERTS, 2 * D_FF), 0.01),
        'w_down': nrm(ks[28], (L_, N_EXPERTS, D_FF, D_MODEL), D_FF ** -0.5),
        'b_down': nrm(ks[29], (L_, N_EXPERTS, D_MODEL), 0.01),
    }


def reference(x_prompt, x_sample, g_mix, w_in, g_q, g_k, attn_sinks, w_short, b_short,
              flt_w1, flt_b1, flt_freq1, flt_w2, flt_b2, flt_freq2, flt_w3, flt_b3, flt_freq3,
              flt_w4, hyena_skip, g_attn_out, g_hyena_out, w_out, g_ffn,
              w_router, b_router, w_gate_up, b_gate_up, w_down, b_down):
    def trunk(x):
        for i in range(DEPTH):
            x = encoder_layer(x, g_mix[i], w_in[i], g_q[i], g_k[i], attn_sinks[i],
                              w_short[i], b_short[i],
                              flt_w1[i], flt_b1[i], flt_freq1[i], flt_w2[i], flt_b2[i], flt_freq2[i],
                              flt_w3[i], flt_b3[i], flt_freq3[i], flt_w4[i], hyena_skip[i],
                              g_attn_out[i], g_hyena_out[i], w_out[i], g_ffn[i],
                              w_router[i], b_router[i], w_gate_up[i], b_gate_up[i],
                              w_down[i], b_down[i])
        return x

    y_prompt = trunk(x_prompt)
    y_sample = trunk(x_sample)
    return (y_prompt, y_sample)
```

```python
import functools
import math

import numpy as np
import jax
import jax.numpy as jnp
from jax import lax
from jax.experimental import pallas as pl
from jax.experimental.pallas import tpu as pltpu

HEAD_DIM = 64
WINDOW = 128
TOP_K = 4
EPS = 1e-6
POS_BANDS = 16
FEAT_PAD = 64
DECAY_FAST = 0.3
DECAY_SLOW = 1.5
DECAY_TARGET = 1e-2
SWIGLU_ALPHA = 1.702
SWIGLU_LIMIT = 7.0

LANES = 128
SUBLANES = 8
VMEM_LIMIT = 56 * 1024 * 1024
MOE_ROWS = 512
F32 = jnp.float32
BF16 = jnp.bfloat16
NT_DIMS = (((1,), (1,)), ((), ()))


def _cparams(sem):
    return pltpu.CompilerParams(dimension_semantics=sem, vmem_limit_bytes=VMEM_LIMIT)


def _split_bf16(a):
    hi = a.astype(BF16)
    lo = (a - hi.astype(F32)).astype(BF16)
    return hi, lo


def _dot3(a, b):
    ah, al = _split_bf16(a)
    bh, bl = _split_bf16(b)
    d = functools.partial(jnp.dot, preferred_element_type=F32)
    return d(ah, bh) + d(ah, bl) + d(al, bh)


def _inproj_kernel(x_ref, g_ref, w_ref, gq_ref, gk_ref, s_ref, o_ref, h_scr, *, n_q_tiles):
    j = pl.program_id(1)

    @pl.when(j == 0)
    def _():
        x = x_ref[...]
        ms = jnp.mean(x * x, axis=-1, keepdims=True)
        h_scr[...] = (x * lax.rsqrt(ms + EPS) * g_ref[...]).astype(BF16)

    acc = jnp.dot(h_scr[...], w_ref[...], preferred_element_type=F32)
    tn = acc.shape[1]

    def head_norm(a, g):
        hi, lo = _split_bf16(a * a)
        s = s_ref[:a.shape[1], :a.shape[1]]
        ss = (jnp.dot(hi, s, preferred_element_type=F32)
              + jnp.dot(lo, s, preferred_element_type=F32))
        return a * lax.rsqrt(ss * (1.0 / HEAD_DIM) + EPS) * g

    @pl.when(j < n_q_tiles)
    def _():
        o_ref[...] = head_norm(acc, gq_ref[...]).astype(o_ref.dtype)

    @pl.when(j == n_q_tiles)
    def _():
        half = tn // 2
        o_ref[:, :half] = head_norm(acc[:, :half], gk_ref[...]).astype(o_ref.dtype)
        o_ref[:, half:] = acc[:, half:].astype(o_ref.dtype)

    @pl.when(j > n_q_tiles)
    def _():
        o_ref[...] = acc.astype(o_ref.dtype)


def _inproj(x2d, g_mix, w_in_bf, g_q, g_k, attn_w, kv_w):
    t, d = x2d.shape
    in_w = w_in_bf.shape[1]
    tn = 2 * kv_w
    tm = min(1024, t)
    assert attn_w % tn == 0 and in_w % tn == 0 and t % tm == 0
    n_q_tiles = attn_w // tn
    heads_per_tile = tn // HEAD_DIM
    blk = np.kron(np.eye(heads_per_tile, dtype=np.float32), np.ones((HEAD_DIM, HEAD_DIM), np.float32))
    gq_t = jnp.tile(g_q.reshape(1, HEAD_DIM), (1, heads_per_tile))
    gk_t = jnp.tile(g_k.reshape(1, HEAD_DIM), (1, heads_per_tile // 2))
    return pl.pallas_call(
        functools.partial(_inproj_kernel, n_q_tiles=n_q_tiles),
        grid=(t // tm, in_w // tn),
        in_specs=[
            pl.BlockSpec((tm, d), lambda i, j: (i, 0)),
            pl.BlockSpec((1, d), lambda i, j: (0, 0)),
            pl.BlockSpec((d, tn), lambda i, j: (0, j)),
            pl.BlockSpec((1, tn), lambda i, j: (0, 0)),
            pl.BlockSpec((1, tn // 2), lambda i, j: (0, 0)),
            pl.BlockSpec((tn, tn), lambda i, j: (0, 0)),
        ],
        out_specs=pl.BlockSpec((tm, tn), lambda i, j: (i, j)),
        out_shape=jax.ShapeDtypeStruct((t, in_w), BF16),
        scratch_shapes=[pltpu.VMEM((tm, d), BF16)],
        compiler_params=_cparams(("arbitrary", "arbitrary")),
        name="inproj",
    )(x2d, g_mix.reshape(1, d), w_in_bf, gq_t, gk_t, jnp.asarray(blk, BF16))


def _attn_kernel(q_ref, k_ref, v_ref, sink_ref, g_ref, o_ref, o_scr, *, n_heads, n_kv, tq, span, seq):
    i = pl.program_id(1)
    grp = n_heads // n_kv
    start = pl.multiple_of(jnp.clip(i * tq - WINDOW, 0, seq - span), WINDOW)
    qpos = i * tq + lax.broadcasted_iota(jnp.int32, (tq, span), 0)
    kpos = start + lax.broadcasted_iota(jnp.int32, (tq, span), 1)
    dist = jnp.abs(qpos - kpos)
    valid = dist <= WINDOW
    distf = dist.astype(F32)
    for kh in range(n_kv):
        kk = k_ref[0, pl.ds(start, span), kh * HEAD_DIM:(kh + 1) * HEAD_DIM]
        vv = v_ref[0, pl.ds(start, span), kh * HEAD_DIM:(kh + 1) * HEAD_DIM]
        for g in range(grp):
            h = kh * grp + g
            slope = 2.0 ** (-8.0 * (h + 1) / n_heads)
            q = q_ref[0, :, h * HEAD_DIM:(h + 1) * HEAD_DIM]
            s = lax.dot_general(q, kk, NT_DIMS, preferred_element_type=F32) * (HEAD_DIM ** -0.5)
            s = s - slope * distf
            s = jnp.where(valid, s, -jnp.inf)
            sink = sink_ref[h]
            m = jnp.maximum(jnp.max(s, axis=-1, keepdims=True), sink)
            p = jnp.exp(s - m)
            denom = jnp.sum(p, axis=-1, keepdims=True) + jnp.exp(sink - m)
            o = jnp.dot(p.astype(BF16), vv, preferred_element_type=F32)
            o_scr[:, h * HEAD_DIM:(h + 1) * HEAD_DIM] = o / denom
    a = o_scr[...]
    ms = jnp.mean(a * a, axis=-1, keepdims=True)
    o_ref[0] = (a * lax.rsqrt(ms + EPS) * g_ref[...]).astype(o_ref.dtype)


def _attention(proj3, sinks, g_attn_out, n_heads, kv_w):
    b, seq, _ = proj3.shape
    attn_w = n_heads * HEAD_DIM
    n_kv = kv_w // HEAD_DIM
    tq = WINDOW
    span = min(3 * WINDOW, seq)
    kcol = attn_w // kv_w
    return pl.pallas_call(
        functools.partial(_attn_kernel, n_heads=n_heads, n_kv=n_kv, tq=tq, span=span, seq=seq),
        grid=(b, seq // tq),
        in_specs=[
            pl.BlockSpec((1, tq, attn_w), lambda bi, i: (bi, i, 0)),
            pl.BlockSpec((1, seq, kv_w), lambda bi, i: (bi, 0, kcol)),
            pl.BlockSpec((1, seq, kv_w), lambda bi, i: (bi, 0, kcol + 1)),
            pl.BlockSpec(memory_space=pltpu.SMEM),
            pl.BlockSpec((1, attn_w), lambda bi, i: (0, 0)),
        ],
        out_specs=pl.BlockSpec((1, tq, attn_w), lambda bi, i: (bi, i, 0)),
        out_shape=jax.ShapeDtypeStruct((b, seq, attn_w), BF16),
        scratch_shapes=[pltpu.VMEM((tq, attn_w), F32)],
        compiler_params=_cparams(("arbitrary", "arbitrary")),
        name="attention",
    )(proj3, proj3, proj3, sinks.astype(F32), g_attn_out.reshape(1, attn_w))


def _fft_plan(seq):
    n = 2 * seq
    n2 = 1 << (int(math.log2(n)) // 2)
    n1 = n // n2
    return n, n1, n2


@functools.lru_cache(maxsize=None)
def _dft_constants(seq):
    n, n1, n2 = _fft_plan(seq)
    n1h = n1 // 2

    def blockc(re, im):
        return np.block([[re, -im], [im, re]])

    k1 = np.arange(n1)[:, None]
    a1 = 2.0 * np.pi * (k1 * np.arange(n1h)[None, :]) / n1
    f1 = blockc(np.cos(a1), -np.sin(a1))
    f1i = blockc(np.cos(a1.T), np.sin(a1.T))
    kk1 = np.arange(n1)[:, None, None]
    kk2 = np.arange(n2)[None, :, None]
    nn2 = np.arange(n2)[None, None, :]
    a2 = 2.0 * np.pi * (nn2 * (kk1 + n1 * kk2)) / n
    c2, s2 = np.cos(a2), np.sin(a2)
    g2 = np.concatenate([np.concatenate([c2, s2], axis=2),
                         np.concatenate([-s2, c2], axis=2)], axis=1)
    c2t, s2t = np.swapaxes(c2, 1, 2), np.swapaxes(s2, 1, 2)
    g2i = np.concatenate([np.concatenate([c2t, -s2t], axis=2),
                          np.concatenate([s2t, c2t], axis=2)], axis=1)
    return (f1.astype(np.float32), f1i.astype(np.float32),
            g2.astype(np.float32), g2i.astype(np.float32))


def _pack_complex(re, im):
    rb = lax.bitcast_convert_type(re.astype(BF16).astype(F32), jnp.uint32)
    ib = lax.bitcast_convert_type(im.astype(BF16).astype(F32), jnp.uint32)
    return (rb & jnp.uint32(0xFFFF0000)) | (ib >> 16)


def _unpack_complex(w):
    re = lax.bitcast_convert_type(w & jnp.uint32(0xFFFF0000), F32).astype(BF16)
    im = lax.bitcast_convert_type(w << 16, F32).astype(BF16)
    return re, im


def _fft_stage1(load_slab, f1_ref, a_scr, n1, n2):
    def body(s, carry):
        a = jnp.dot(f1_ref[...], load_slab(s), preferred_element_type=F32)
        a_scr[pl.ds(pl.multiple_of(s * n1, n1), n1), :] = _pack_complex(a[:n1], a[n1:])
        return carry
    lax.fori_loop(0, n2, body, 0)


def _load_freq_rows(a_scr, k1, n1, n2):
    re, im = _unpack_complex(a_scr[pl.ds(k1, n2, stride=n1), :])
    return jnp.concatenate([re, im], axis=0)


def _shortconv_kernel(p_ref, w_ref, b_ref, o_ref, u_scr, *, seq, n2, chunk):
    n1h = seq // n2
    w0, w1, w2 = w_ref[0:1, :], w_ref[1:2, :], w_ref[2:3, :]
    bias = b_ref[...]
    halo = 16
    for c in range(seq // chunk):
        r0 = c * chunk
        lo = max(r0 - halo, 0)
        hi = min(r0 + chunk + halo, seq)
        win = p_ref[0, lo:hi, :].astype(F32)
        off = r0 - lo
        cur = win[off:off + chunk]
        if r0 > 0:
            prev = win[off - 1:off - 1 + chunk]
        else:
            prev = jnp.concatenate([jnp.zeros((1, LANES), F32), win[0:chunk - 1]], axis=0)
        if r0 + chunk < seq:
            nxt = win[off + 1:off + 1 + chunk]
        else:
            nxt = jnp.concatenate([win[off + 1:off + chunk], jnp.zeros((1, LANES), F32)], axis=0)
        u_scr[r0:r0 + chunk, :] = prev * w0 + cur * w1 + nxt * w2 + bias

    def body(s, carry):
        o_ref[0, s] = u_scr[pl.ds(s, n1h, stride=n2), :].astype(o_ref.dtype)
        return carry
    lax.fori_loop(0, n2, body, 0)


def _shortconv(proj3, w_short, b_short, col0):
    b, seq, _ = proj3.shape
    cu = w_short.shape[1]
    _, n1, n2 = _fft_plan(seq)
    n1h = n1 // 2
    chunk = min(512, seq)
    cb0 = col0 // LANES
    return pl.pallas_call(
        functools.partial(_shortconv_kernel, seq=seq, n2=n2, chunk=chunk),
        grid=(b, cu // LANES),
        in_specs=[
            pl.BlockSpec((1, seq, LANES), lambda bi, c: (bi, 0, cb0 + c)),
            pl.BlockSpec((3, LANES), lambda bi, c: (0, c)),
            pl.BlockSpec((1, LANES), lambda bi, c: (0, c)),
        ],
        out_specs=pl.BlockSpec((1, n2, n1h, LANES), lambda bi, c: (bi, 0, 0, c)),
        out_shape=jax.ShapeDtypeStruct((b, n2, n1h, cu), BF16),
        scratch_shapes=[pltpu.VMEM((seq, LANES), F32)],
        compiler_params=_cparams(("arbitrary", "arbitrary")),
        name="shortconv",
    )(proj3, w_short, b_short.reshape(1, cu))


def _filtergen_kernel(z_ref, w1_ref, b1_ref, f1_ref, w2_ref, b2_ref, f2_ref, w3_ref, b3_ref, f3_ref,
                      w4_ref, dl_ref, o_ref, h_scr):
    j = pl.program_id(1)

    @pl.when(j == 0)
    def _():
        z = z_ref[...]
        h = jnp.sin(f1_ref[...] * (_dot3(z, w1_ref[...]) + b1_ref[...]))
        h = jnp.sin(f2_ref[...] * (_dot3(h, w2_ref[...]) + b2_ref[...]))
        h_scr[...] = jnp.sin(f3_ref[...] * (_dot3(h, w3_ref[...]) + b3_ref[...]))

    t01 = z_ref[:, 0:1]
    decay = jnp.exp(-t01 * dl_ref[...])
    o_ref[...] = _dot3(h_scr[...], w4_ref[...]) * decay


@functools.lru_cache(maxsize=None)
def _filter_features(seq):
    _, n1, n2 = _fft_plan(seq)
    n1h = n1 // 2
    t01 = np.linspace(0.0, 1.0, seq, dtype=np.float32)[:, None]
    ang = (2.0 * math.pi * np.arange(seq, dtype=np.float32)[:, None] / seq).astype(np.float32)
    bands = np.linspace(1e-4, POS_BANDS - 1, POS_BANDS, dtype=np.float32)[None, :]
    arg = (bands * ang).astype(np.float32)
    z = np.concatenate([t01, np.cos(arg), -np.sin(arg)], axis=-1).astype(np.float32)
    pos = (np.arange(n1h)[None, :] * n2 + np.arange(n2)[:, None]).reshape(-1)
    zp = np.zeros((seq, FEAT_PAD), np.float32)
    zp[:, :z.shape[1]] = z[pos]
    return zp


def _filtergen(seq, w1, b1, f1, w2, b2, f2, w3, b3, f3, w4, hy_ch):
    fw = w1.shape[1]
    fo = w4.shape[1]
    z = jnp.asarray(_filter_features(seq))
    pe = z.shape[1]
    w1 = jnp.pad(w1, ((0, pe - w1.shape[0]), (0, 0)))
    deltas = np.abs(np.linspace(math.log(DECAY_FAST) / DECAY_TARGET, math.log(DECAY_SLOW) / DECAY_TARGET,
                                hy_ch, dtype=np.float32))
    dl = jnp.asarray(np.tile(deltas, fo // hy_ch).reshape(1, fo))
    tr = min(512, seq)
    tc = min(512, fo)
    row = lambda a: a.reshape(1, -1)
    full = lambda shp: pl.BlockSpec(shp, lambda i, j: (0, 0))
    return pl.pallas_call(
        _filtergen_kernel,
        grid=(seq // tr, fo // tc),
        in_specs=[
            pl.BlockSpec((tr, pe), lambda i, j: (i, 0)),
            full((pe, fw)), full((1, fw)), full((1, fw)),
            full((fw, fw)), full((1, fw)), full((1, fw)),
            full((fw, fw)), full((1, fw)), full((1, fw)),
            pl.BlockSpec((fw, tc), lambda i, j: (0, j)),
            pl.BlockSpec((1, tc), lambda i, j: (0, j)),
        ],
        out_specs=pl.BlockSpec((tr, tc), lambda i, j: (i, j)),
        out_shape=jax.ShapeDtypeStruct((seq, fo), F32),
        scratch_shapes=[pltpu.VMEM((tr, fw), F32)],
        compiler_params=_cparams(("arbitrary", "arbitrary")),
        name="filtergen",
    )(z, w1, row(b1), row(f1), w2, row(b2), row(f2), w3, row(b3), row(f3), w4, dl)


def _fftfwd_kernel(h_ref, f1_ref, g2_ref, o_ref, a_scr, *, n1, n2, k1c, scale):
    kc = pl.program_id(1)

    @pl.when(kc == 0)
    def _():
        _fft_stage1(lambda s: h_ref[s].astype(BF16), f1_ref, a_scr, n1, n2)

    def body(kk, carry):
        k1 = kc * k1c + kk
        x = jnp.dot(g2_ref[kk], _load_freq_rows(a_scr, k1, n1, n2), preferred_element_type=F32)
        o_ref[kk] = x * scale
        return carry
    lax.fori_loop(0, k1c, body, 0)


def _fftfwd_real(hs, seq):
    n, n1, n2 = _fft_plan(seq)
    n1h = n1 // 2
    cols = hs.shape[-1]
    f1, _, g2, _ = _dft_constants(seq)
    f1r = jnp.asarray(f1[:, :n1h], BF16)
    g2b = jnp.asarray(g2, BF16)
    k1c = min(8, n1)
    return pl.pallas_call(
        functools.partial(_fftfwd_kernel, n1=n1, n2=n2, k1c=k1c, scale=1.0 / n),
        grid=(cols // LANES, n1 // k1c),
        in_specs=[
            pl.BlockSpec((n2, n1h, LANES), lambda c, k: (0, 0, c)),
            pl.BlockSpec((2 * n1, n1h), lambda c, k: (0, 0)),
            pl.BlockSpec((k1c, 2 * n2, 2 * n2), lambda c, k: (k, 0, 0)),
        ],
        out_specs=pl.BlockSpec((k1c, 2 * n2, LANES), lambda c, k: (k, 0, c)),
        out_shape=jax.ShapeDtypeStruct((n1, 2 * n2, cols), F32),
        scratch_shapes=[pltpu.VMEM((n2 * n1, LANES), jnp.uint32)],
        compiler_params=_cparams(("arbitrary", "arbitrary")),
        name="fftfwd",
    )(hs.reshape(n2, n1h, cols), f1r, g2b)


def _hyena_kernel(u_ref, g_ref, skip_ref, hf_ref, hb_ref, f1_ref, f1i_ref, g2_ref, g2i_ref, o_ref, a_scr,
                  *, n1, n2, k1c, token_order):
    kc = pl.program_id(2)
    n1h = n1 // 2

    @pl.when(kc == 0)
    def _():
        _fft_stage1(lambda s: jnp.concatenate([u_ref[0, s], u_ref[1, s]], axis=0), f1_ref, a_scr, n1, n2)

    def body(kk, carry):
        k1 = kc * k1c + kk
        x = jnp.dot(g2_ref[kk], _load_freq_rows(a_scr, k1, n1, n2), preferred_element_type=F32)
        xr, xi = x[:n2], x[n2:]
        hf = hf_ref[kk]
        hb = hb_ref[kk]
        hr = hf[:n2] + hb[:n2]
        hi = hf[n2:] - hb[n2:]
        y = jnp.concatenate([xr * hr - xi * hi, xr * hi + xi * hr], axis=0).astype(BF16)
        bq = jnp.dot(g2i_ref[kk], y, preferred_element_type=F32)
        a_scr[pl.ds(k1, n2, stride=n1), :] = _pack_complex(bq[:n2], bq[n2:])
        return carry
    lax.fori_loop(0, k1c, body, 0)

    @pl.when(kc == pl.num_programs(2) - 1)
    def _():
        skip = skip_ref[...]

        def body3(s, carry):
            re, im = _unpack_complex(a_scr[pl.ds(pl.multiple_of(s * n1, n1), n1), :])
            y = jnp.dot(f1i_ref[...], jnp.concatenate([re, im], axis=0), preferred_element_type=F32)
            for bb in range(2):
                yb = y[bb * n1h:(bb + 1) * n1h]
                z = g_ref[bb, s].astype(F32) * (yb + skip * u_ref[bb, s].astype(F32))
                if token_order:
                    o_ref[bb, pl.ds(s, n1h, stride=n2), :] = z.astype(o_ref.dtype)
                else:
                    o_ref[bb, s] = z.astype(o_ref.dtype)
            return carry
        lax.fori_loop(0, n2, body3, 0)


def _hyena_conv(u_arr, u_col, g_arr, g_col, skip, tf, order, seq, hy_ch, token_order):
    b = u_arr.shape[0]
    n, n1, n2 = _fft_plan(seq)
    n1h = n1 // 2
    f1, f1i, g2, g2i = _dft_constants(seq)
    k1c = min(8, n1)
    ucb, gcb = u_col // LANES, g_col // LANES
    fcb, bcb = (2 * order) * hy_ch // LANES, (2 * order + 1) * hy_ch // LANES
    if token_order:
        out_spec = pl.BlockSpec((2, seq, LANES), lambda c, p, k: (p, 0, c))
        out_shape = jax.ShapeDtypeStruct((b, seq, hy_ch), F32)
    else:
        out_spec = pl.BlockSpec((2, n2, n1h, LANES), lambda c, p, k: (p, 0, 0, c))
        out_shape = jax.ShapeDtypeStruct((b, n2, n1h, hy_ch), BF16)
    return pl.pallas_call(
        functools.partial(_hyena_kernel, n1=n1, n2=n2, k1c=k1c, token_order=token_order),
        grid=(hy_ch // LANES, b // 2, n1 // k1c),
        in_specs=[
            pl.BlockSpec((2, n2, n1h, LANES), lambda c, p, k: (p, 0, 0, ucb + c)),
            pl.BlockSpec((2, n2, n1h, LANES), lambda c, p, k: (p, 0, 0, gcb + c)),
            pl.BlockSpec((1, LANES), lambda c, p, k: (0, c)),
            pl.BlockSpec((k1c, 2 * n2, LANES), lambda c, p, k: (k, 0, fcb + c)),
            pl.BlockSpec((k1c, 2 * n2, LANES), lambda c, p, k: (k, 0, bcb + c)),
            pl.BlockSpec((2 * n1, n1), lambda c, p, k: (0, 0)),
            pl.BlockSpec((n1, 2 * n1), lambda c, p, k: (0, 0)),
            pl.BlockSpec((k1c, 2 * n2, 2 * n2), lambda c, p, k: (k, 0, 0)),
            pl.BlockSpec((k1c, 2 * n2, 2 * n2), lambda c, p, k: (k, 0, 0)),
        ],
        out_specs=out_spec,
        out_shape=out_shape,
        scratch_shapes=[pltpu.VMEM((n2 * n1, LANES), jnp.uint32)],
        compiler_params=_cparams(("arbitrary", "arbitrary", "arbitrary")),
        name="hyenaconv",
    )(u_arr, g_arr, skip.reshape(1, hy_ch), tf, tf,
      jnp.asarray(f1, BF16), jnp.asarray(f1i, BF16), jnp.asarray(g2, BF16), jnp.asarray(g2i, BF16))


def _outproj_kernel(a_ref, z_ref, x_ref, wa_ref, wh_ref, gh_ref, gf_ref, wrh_ref, wrl_ref, br_ref, c0_ref,
                    tri_ref, x1_ref, xn_ref, idx_ref, gate_ref, rank_ref, cnt_ref, run_scr, *, n_exp):
    i = pl.program_id(0)

    @pl.when(i == 0)
    def _():
        run_scr[...] = c0_ref[...]

    z = z_ref[...]
    zn = (z * lax.rsqrt(jnp.mean(z * z, axis=-1, keepdims=True) + EPS) * gh_ref[...]).astype(BF16)
    mixed = (jnp.dot(a_ref[...], wa_ref[...], preferred_element_type=F32)
             + jnp.dot(zn, wh_ref[...], preferred_element_type=F32))
    x1 = x_ref[...] + mixed
    x1_ref[...] = x1
    xn = x1 * lax.rsqrt(jnp.mean(x1 * x1, axis=-1, keepdims=True) + EPS) * gf_ref[...]
    half = xn.shape[1] // 2
    packed = _pack_complex(xn[:, :half], xn[:, half:])
    for c in range(SUBLANES):
        xn_ref[pl.ds(c, xn.shape[0], stride=SUBLANES), :] = packed[:, c * LANES:(c + 1) * LANES]

    xh, xl = _split_bf16(xn)
    dg = functools.partial(lax.dot_general, dimension_numbers=NT_DIMS, preferred_element_type=F32)
    logit = dg(wrh_ref[...], xh) + dg(wrh_ref[...], xl) + dg(wrl_ref[...], xh) + br_ref[...]
    tm = logit.shape[1]
    eio = lax.broadcasted_iota(jnp.int32, (n_exp, tm), 0)
    vals, onehots = [], []
    cur = logit
    for k in range(TOP_K):
        m = jnp.max(cur, axis=0, keepdims=True)
        idx = jnp.min(jnp.where(cur == m, eio, n_exp), axis=0, keepdims=True)
        oh = eio == idx
        vals.append(m)
        onehots.append(oh)
        idx_ref[k:k + 1, :] = idx
        cur = jnp.where(oh, -jnp.inf, cur)
    ex = [jnp.exp(v - vals[0]) for v in vals]
    tot = ex[0] + ex[1] + ex[2] + ex[3]
    for k in range(TOP_K):
        gate_ref[k:k + 1, :] = ex[k] / tot
    ohsum = jnp.zeros((n_exp, tm), F32)
    for oh in onehots:
        ohsum = ohsum + oh.astype(F32)
    base = jnp.dot(ohsum.astype(BF16), tri_ref[...], preferred_element_type=F32) + run_scr[...]
    for k in range(TOP_K):
        rank_ref[k:k + 1, :] = jnp.sum(jnp.where(onehots[k], base, 0.0), axis=0, keepdims=True).astype(jnp.int32)
    run_scr[...] = run_scr[...] + jnp.sum(ohsum, axis=1, keepdims=True)
    cnt_ref[...] = run_scr[...]


def _outproj_router(attn_n, z2, x2d, wa, wh, g_hy, g_ffn, wr_hi, wr_lo, b_router, count0):
    t, d = x2d.shape
    aw, hc = attn_n.shape[1], z2.shape[1]
    n_exp = wr_hi.shape[0]
    tm = min(256, t)
    tri = jnp.asarray(np.triu(np.ones((tm, tm), np.float32), k=1), BF16)
    full = lambda shp: pl.BlockSpec(shp, lambda i: (0, 0))
    rowblk = lambda w: pl.BlockSpec((tm, w), lambda i: (i, 0))
    colblk = pl.BlockSpec((TOP_K, tm), lambda i: (0, i))
    return pl.pallas_call(
        functools.partial(_outproj_kernel, n_exp=n_exp),
        grid=(t // tm,),
        in_specs=[rowblk(aw), rowblk(hc), rowblk(d), full((aw, d)), full((hc, d)), full((1, hc)), full((1, d)),
                  full((n_exp, d)), full((n_exp, d)), full((n_exp, 1)), full((n_exp, 1)), full((tm, tm))],
        out_specs=[rowblk(d), pl.BlockSpec((tm * SUBLANES, LANES), lambda i: (i, 0)),
                   colblk, colblk, colblk, full((n_exp, 1))],
        out_shape=[jax.ShapeDtypeStruct((t, d), F32), jax.ShapeDtypeStruct((t * SUBLANES, LANES), jnp.uint32),
                   jax.ShapeDtypeStruct((TOP_K, t), jnp.int32), jax.ShapeDtypeStruct((TOP_K, t), F32),
                   jax.ShapeDtypeStruct((TOP_K, t), jnp.int32), jax.ShapeDtypeStruct((n_exp, 1), F32)],
        scratch_shapes=[pltpu.VMEM((n_exp, 1), F32)],
        compiler_params=_cparams(("arbitrary",)),
        name="outproj_router",
    )(attn_n, z2, x2d, wa, wh, g_hy.reshape(1, hc), g_ffn.reshape(1, d), wr_hi, wr_lo,
      b_router.reshape(n_exp, 1), count0, tri)


def _dest_kernel(ps_ref, idx_ref, rank_ref, o_ref, *, n_exp):
    idx = idx_ref[...]
    acc = rank_ref[...]
    for e in range(n_exp):
        acc = acc + jnp.where(idx == e, ps_ref[e], 0)
    o_ref[...] = acc


def _dest_rows(pstart, idx, rank):
    k, t = idx.shape
    tb = min(2048, t)
    blk = pl.BlockSpec((k, tb), lambda i: (0, i))
    return pl.pallas_call(
        functools.partial(_dest_kernel, n_exp=pstart.shape[0]),
        grid=(t // tb,),
        in_specs=[pl.BlockSpec(memory_space=pltpu.SMEM), blk, blk],
        out_specs=blk,
        out_shape=jax.ShapeDtypeStruct((k, t), jnp.int32),
        compiler_params=_cparams(("arbitrary",)),
        name="dest",
    )(pstart, idx, rank)


def _dispatch_kernel(dest_ref, xn_ref, xs_in_ref, xs_ref, sem):
    del xs_in_ref
    tm = xn_ref.shape[0] // SUBLANES

    def row_copy(t, d):
        src = xn_ref.at[pl.ds(pl.multiple_of(t * SUBLANES, SUBLANES), SUBLANES)]
        return pltpu.make_async_copy(src, xs_ref.at[d], sem)

    def issue(t, carry):
        for k in range(TOP_K):
            row_copy(t, dest_ref[k, t]).start()
        return carry
    lax.fori_loop(0, tm, issue, 0)

    def drain(t, carry):
        for k in range(TOP_K):
            row_copy(0, 0).wait()
        return carry
    lax.fori_loop(0, tm, drain, 0)


def _dispatch(dest, xn, xs):
    t = dest.shape[1]
    tm = min(256, t)
    return pl.pallas_call(
        _dispatch_kernel,
        grid=(t // tm,),
        in_specs=[pl.BlockSpec((TOP_K, tm), lambda i: (0, i), memory_space=pltpu.SMEM),
                  pl.BlockSpec((tm * SUBLANES, LANES), lambda i: (i, 0)),
                  pl.BlockSpec(memory_space=pl.ANY)],
        out_specs=pl.BlockSpec(memory_space=pl.ANY),
        out_shape=jax.ShapeDtypeStruct(xs.shape, xs.dtype),
        scratch_shapes=[pltpu.SemaphoreType.DMA(())],
        input_output_aliases={2: 0},
        compiler_params=_cparams(("arbitrary",)),
        name="dispatch",
    )(dest, xn, xs)


def _expert_kernel(be_ref, nu_ref, x_ref, wg_ref, wu_ref, bg_ref, bu_ref, wd_ref, bd_ref, y_ref, x_scr, acc_scr):
    del be_ref
    i, j = pl.program_id(0), pl.program_id(1)
    bm, d = x_scr.shape
    half = d // 2
    out_rows = d // LANES

    @pl.when(i < nu_ref[0])
    def _():
        @pl.when(j == 0)
        def _():
            for c in range(SUBLANES):
                hi, lo = _unpack_complex(x_ref[pl.ds(c, bm, stride=SUBLANES), :])
                x_scr[:, c * LANES:(c + 1) * LANES] = hi
                x_scr[:, half + c * LANES:half + (c + 1) * LANES] = lo

        x = x_scr[...]
        g = jnp.dot(x, wg_ref[0], preferred_element_type=F32) + bg_ref[0]
        u = jnp.dot(x, wu_ref[0], preferred_element_type=F32) + bu_ref[0]
        glu = jnp.minimum(g, SWIGLU_LIMIT)
        lin = jnp.clip(u, -SWIGLU_LIMIT, SWIGLU_LIMIT)
        h = glu * jax.nn.sigmoid(SWIGLU_ALPHA * glu) * (lin + 1.0)
        part = jnp.dot(h.astype(BF16), wd_ref[0], preferred_element_type=F32)

        @pl.when(j == 0)
        def _():
            acc_scr[...] = part + bd_ref[0]

        @pl.when(j > 0)
        def _():
            acc_scr[...] = acc_scr[...] + part

        @pl.when(j == pl.num_programs(1) - 1)
        def _():
            for c in range(out_rows):
                y_ref[pl.ds(c, bm, stride=out_rows), :] = acc_scr[:, c * LANES:(c + 1) * LANES]


def _experts(block_expert, n_used, xs, w_gu, b_gu, w_dn, b_dn):
    n_exp, d, f2 = w_gu.shape
    n_rows = xs.shape[0] // SUBLANES
    out_rows = d // LANES
    f = f2 // 2
    tf = 512 if f % 512 == 0 else 256
    assert f % tf == 0
    nf = f // tf
    bm = MOE_ROWS
    last = lambda i, nu: jnp.minimum(i, nu[0] - 1)
    fcol = lambda i, j, nu: jnp.where(i < nu[0], j, nf - 1)
    grid_spec = pltpu.PrefetchScalarGridSpec(
        num_scalar_prefetch=2,
        grid=(n_rows // bm, nf),
        in_specs=[
            pl.BlockSpec((bm * SUBLANES, LANES), lambda i, j, be, nu: (last(i, nu), 0)),
            pl.BlockSpec((1, d, tf), lambda i, j, be, nu: (be[i], 0, fcol(i, j, nu))),
            pl.BlockSpec((1, d, tf), lambda i, j, be, nu: (be[i], 0, nf + fcol(i, j, nu))),
            pl.BlockSpec((1, 1, tf), lambda i, j, be, nu: (be[i], 0, fcol(i, j, nu))),
            pl.BlockSpec((1, 1, tf), lambda i, j, be, nu: (be[i], 0, nf + fcol(i, j, nu))),
            pl.BlockSpec((1, tf, d), lambda i, j, be, nu: (be[i], fcol(i, j, nu), 0)),
            pl.BlockSpec((1, 1, d), lambda i, j, be, nu: (be[i], 0, 0)),
        ],
        out_specs=pl.BlockSpec((bm * out_rows, LANES), lambda i, j, be, nu: (last(i, nu), 0)),
        scratch_shapes=[pltpu.VMEM((bm, d), BF16), pltpu.VMEM((bm, d), F32)],
    )
    return pl.pallas_call(
        _expert_kernel,
        grid_spec=grid_spec,
        out_shape=jax.ShapeDtypeStruct((n_rows * out_rows, LANES), F32),
        compiler_params=_cparams(("arbitrary", "arbitrary")),
        name="experts",
    )(block_expert, n_used, xs, w_gu, w_gu, b_gu.reshape(n_exp, 1, f2), b_gu.reshape(n_exp, 1, f2),
      w_dn, b_dn.reshape(n_exp, 1, d))


def _combine_kernel(dest_ref, x1_ref, gate_ref, ys_ref, o_ref, buf, sem):
    tm, d = x1_ref.shape
    out_rows = d // LANES

    def row_copy(t, k, d_row):
        dst = buf.at[k, pl.ds(pl.multiple_of(t * out_rows, out_rows), out_rows)]
        return pltpu.make_async_copy(ys_ref.at[d_row], dst, sem)

    def issue(t, carry):
        for k in range(TOP_K):
            row_copy(t, k, dest_ref[k, t]).start()
        return carry
    lax.fori_loop(0, tm, issue, 0)

    def drain(t, carry):
        for k in range(TOP_K):
            row_copy(0, 0, 0).wait()
        return carry
    lax.fori_loop(0, tm, drain, 0)

    gate = gate_ref[...]
    for c in range(out_rows):
        moe = gate[:, 0:1] * buf[0, pl.ds(c, tm, stride=out_rows), :]
        for k in range(1, TOP_K):
            moe = moe + gate[:, k:k + 1] * buf[k, pl.ds(c, tm, stride=out_rows), :]
        o_ref[:, c * LANES:(c + 1) * LANES] = x1_ref[:, c * LANES:(c + 1) * LANES] + moe


def _combine(dest, x1, gate_tk, ys):
    t, d = x1.shape
    tm = min(128, t)
    out_rows = d // LANES
    return pl.pallas_call(
        _combine_kernel,
        grid=(t // tm,),
        in_specs=[pl.BlockSpec((TOP_K, tm), lambda i: (0, i), memory_space=pltpu.SMEM),
                  pl.BlockSpec((tm, d), lambda i: (i, 0)),
                  pl.BlockSpec((tm, TOP_K), lambda i: (i, 0)),
                  pl.BlockSpec(memory_space=pl.ANY)],
        out_specs=pl.BlockSpec((tm, d), lambda i: (i, 0)),
        out_shape=jax.ShapeDtypeStruct((t, d), F32),
        scratch_shapes=[pltpu.VMEM((TOP_K, tm * out_rows, LANES), F32), pltpu.SemaphoreType.DMA(())],
        compiler_params=_cparams(("arbitrary",)),
        name="combine",
    )(dest, x1, gate_tk, ys)


def _mixer(x3, g_mix, w_in_bf, g_q, g_k, sinks, w_short, b_short, flt, hyena_skip, g_attn_out, dims):
    n_heads, kv_w, hy_ch = dims
    b, seq, d = x3.shape
    attn_w = n_heads * HEAD_DIM
    proj = _inproj(x3.reshape(b * seq, d), g_mix, w_in_bf, g_q, g_k, attn_w, kv_w)
    proj3 = proj.reshape(b, seq, proj.shape[1])
    attn_n = _attention(proj3, sinks, g_attn_out, n_heads, kv_w)
    u = _shortconv(proj3, w_short, b_short, attn_w + 2 * kv_w)
    hs = _filtergen(seq, *flt, hy_ch)
    tf = _fftfwd_real(hs, seq)
    z1 = _hyena_conv(u, 0, u, hy_ch, hyena_skip[0], tf, 0, seq, hy_ch, token_order=False)
    z2 = _hyena_conv(z1, 0, u, 2 * hy_ch, hyena_skip[1], tf, 1, seq, hy_ch, token_order=True)
    return attn_n.reshape(b * seq, attn_w), z2.reshape(b * seq, hy_ch)


def kernel(x_prompt, x_sample, g_mix, w_in, g_q, g_k, attn_sinks, w_short, b_short, flt_w1, flt_b1, flt_freq1, flt_w2, flt_b2, flt_freq2, flt_w3, flt_b3, flt_freq3, flt_w4, hyena_skip, g_attn_out, g_hyena_out, w_out, g_ffn, w_router, b_router, w_gate_up, b_gate_up, w_down, b_down):
    assert g_mix.shape[0] == 1, "single-layer trunk"
    d = x_prompt.shape[-1]
    assert d == 2 * SUBLANES * LANES, "routed-row tile layout is built for d_model = 2048"
    n_heads = attn_sinks.shape[-1]
    hy_ch = hyena_skip.shape[-1]
    attn_w = n_heads * HEAD_DIM
    kv_w = (w_in.shape[-1] - attn_w - 3 * hy_ch) // 2
    n_exp = w_router.shape[-1]
    dims = (n_heads, kv_w, hy_ch)

    w_in_bf = w_in[0].astype(BF16)
    wa = w_out[0, :attn_w].astype(BF16)
    wh = w_out[0, attn_w:].astype(BF16)
    wr_t = w_router[0].T
    wr_hi = wr_t.astype(BF16)
    wr_lo = (wr_t - wr_hi.astype(F32)).astype(BF16)
    w_gu = w_gate_up[0].astype(BF16)
    w_dn = w_down[0].astype(BF16)
    flt = (flt_w1[0], flt_b1[0], flt_freq1[0], flt_w2[0], flt_b2[0], flt_freq2[0],
           flt_w3[0], flt_b3[0], flt_freq3[0], flt_w4[0])

    xs_in = (x_prompt, x_sample)
    routed = []
    counts = jnp.zeros((n_exp, 1), F32)
    for x3 in xs_in:
        attn_n, z2 = _mixer(x3, g_mix[0], w_in_bf, g_q[0], g_k[0], attn_sinks[0], w_short[0], b_short[0],
                            flt, hyena_skip[0], g_attn_out[0], dims)
        x1, xn, idx, gate, rank, counts = _outproj_router(
            attn_n, z2, x3.reshape(-1, d), wa, wh, g_hyena_out[0], g_ffn[0], wr_hi, wr_lo, b_router[0], counts)
        routed.append((x1, xn, idx, gate, rank))

    n_slots = sum(r[0].shape[0] for r in routed) * TOP_K
    n_rows = (n_slots // MOE_ROWS + n_exp) * MOE_ROWS
    cnt = counts[:, 0].astype(jnp.int32)
    padded = (cnt + MOE_ROWS - 1) // MOE_ROWS * MOE_ROWS
    pend = jnp.cumsum(padded)
    pstart = (pend - padded).astype(jnp.int32)
    n_blocks = n_rows // MOE_ROWS
    block_expert = jnp.minimum(jnp.searchsorted(pend, jnp.arange(n_blocks) * MOE_ROWS, side='right'),
                               n_exp - 1).astype(jnp.int32)
    n_used = (pend[-1:] // MOE_ROWS).astype(jnp.int32)

    xs = jnp.zeros((n_rows, SUBLANES, LANES), jnp.uint32)
    dests = []
    for (x1, xn, idx, gate, rank) in routed:
        dest = _dest_rows(pstart, idx, rank)
        xs = _dispatch(dest, xn, xs)
        dests.append(dest)
    ys = _experts(block_expert, n_used, xs.reshape(n_rows * SUBLANES, LANES), w_gu, b_gate_up[0], w_dn, b_down[0])
    ys = ys.reshape(n_rows, d // LANES, LANES)
    outs = []
    for (x1, xn, idx, gate, rank), dest, x3 in zip(routed, dests, xs_in):
        outs.append(_combine(dest, x1, gate.T, ys).reshape(x3.shape))
    return tuple(outs)
```

```python
import functools
import math

import numpy as np
import jax
import jax.numpy as jnp
from jax import lax
from jax.experimental import pallas as pl
from jax.experimental.pallas import tpu as pltpu

HEAD_DIM = 64
WINDOW = 128
TOP_K = 4
EPS = 1e-6
POS_BANDS = 16
FEAT_PAD = 64
DECAY_FAST = 0.3
DECAY_SLOW = 1.5
DECAY_TARGET = 1e-2
SWIGLU_ALPHA = 1.702
SWIGLU_LIMIT = 7.0

LANES = 128
SUBLANES = 8
VMEM_LIMIT = 56 * 1024 * 1024
MOE_ROWS = 512
FFT_UNROLL = 4
F32 = jnp.float32
BF16 = jnp.bfloat16
NT_DIMS = (((1,), (1,)), ((), ()))


def _cparams(sem):
    return pltpu.CompilerParams(dimension_semantics=sem, vmem_limit_bytes=VMEM_LIMIT)


def _split_bf16(a):
    hi = a.astype(BF16)
    lo = (a - hi.astype(F32)).astype(BF16)
    return hi, lo


def _dot3(a, b):
    ah, al = _split_bf16(a)
    bh, bl = _split_bf16(b)
    d = functools.partial(jnp.dot, preferred_element_type=F32)
    return d(ah, bh) + d(ah, bl) + d(al, bh)


def _inproj_kernel(x_ref, g_ref, w_ref, gq_ref, gk_ref, s_ref, o_ref, h_scr, *, n_q_tiles):
    j = pl.program_id(1)

    @pl.when(j == 0)
    def _():
        x = x_ref[...]
        ms = jnp.mean(x * x, axis=-1, keepdims=True)
        h_scr[...] = (x * lax.rsqrt(ms + EPS) * g_ref[...]).astype(BF16)

    acc = jnp.dot(h_scr[...], w_ref[...], preferred_element_type=F32)
    tn = acc.shape[1]

    def head_norm(a, g):
        hi, lo = _split_bf16(a * a)
        s = s_ref[:a.shape[1], :a.shape[1]]
        ss = (jnp.dot(hi, s, preferred_element_type=F32)
              + jnp.dot(lo, s, preferred_element_type=F32))
        return a * lax.rsqrt(ss * (1.0 / HEAD_DIM) + EPS) * g

    @pl.when(j < n_q_tiles)
    def _():
        o_ref[...] = head_norm(acc, gq_ref[...]).astype(o_ref.dtype)

    @pl.when(j == n_q_tiles)
    def _():
        half = tn // 2
        o_ref[:, :half] = head_norm(acc[:, :half], gk_ref[...]).astype(o_ref.dtype)
        o_ref[:, half:] = acc[:, half:].astype(o_ref.dtype)

    @pl.when(j > n_q_tiles)
    def _():
        o_ref[...] = acc.astype(o_ref.dtype)


def _inproj(x2d, g_mix, w_in_bf, g_q, g_k, attn_w, kv_w):
    t, d = x2d.shape
    in_w = w_in_bf.shape[1]
    tn = 2 * kv_w
    tm = min(1024, t)
    assert attn_w % tn == 0 and in_w % tn == 0 and t % tm == 0
    n_q_tiles = attn_w // tn
    heads_per_tile = tn // HEAD_DIM
    blk = np.kron(np.eye(heads_per_tile, dtype=np.float32), np.ones((HEAD_DIM, HEAD_DIM), np.float32))
    gq_t = jnp.tile(g_q.reshape(1, HEAD_DIM), (1, heads_per_tile))
    gk_t = jnp.tile(g_k.reshape(1, HEAD_DIM), (1, heads_per_tile // 2))
    return pl.pallas_call(
        functools.partial(_inproj_kernel, n_q_tiles=n_q_tiles),
        grid=(t // tm, in_w // tn),
        in_specs=[
            pl.BlockSpec((tm, d), lambda i, j: (i, 0)),
            pl.BlockSpec((1, d), lambda i, j: (0, 0)),
            pl.BlockSpec((d, tn), lambda i, j: (0, j)),
            pl.BlockSpec((1, tn), lambda i, j: (0, 0)),
            pl.BlockSpec((1, tn // 2), lambda i, j: (0, 0)),
            pl.BlockSpec((tn, tn), lambda i, j: (0, 0)),
        ],
        out_specs=pl.BlockSpec((tm, tn), lambda i, j: (i, j)),
        out_shape=jax.ShapeDtypeStruct((t, in_w), BF16),
        scratch_shapes=[pltpu.VMEM((tm, d), BF16)],
        compiler_params=_cparams(("arbitrary", "arbitrary")),
        name="inproj",
    )(x2d, g_mix.reshape(1, d), w_in_bf, gq_t, gk_t, jnp.asarray(blk, BF16))


def _attn_kernel(q_ref, k_ref, v_ref, sink_ref, g_ref, o_ref, o_scr, *, n_heads, n_kv, tq, span, seq):
    i = pl.program_id(1)
    grp = n_heads // n_kv
    start = pl.multiple_of(jnp.clip(i * tq - WINDOW, 0, seq - span), WINDOW)
    qpos = i * tq + lax.broadcasted_iota(jnp.int32, (tq, span), 0)
    kpos = start + lax.broadcasted_iota(jnp.int32, (tq, span), 1)
    dist = jnp.abs(qpos - kpos)
    valid = dist <= WINDOW
    distf = dist.astype(F32)
    for kh in range(n_kv):
        kk = k_ref[0, pl.ds(start, span), kh * HEAD_DIM:(kh + 1) * HEAD_DIM]
        vv = v_ref[0, pl.ds(start, span), kh * HEAD_DIM:(kh + 1) * HEAD_DIM]
        for g in range(grp):
            h = kh * grp + g
            slope = 2.0 ** (-8.0 * (h + 1) / n_heads)
            q = q_ref[0, :, h * HEAD_DIM:(h + 1) * HEAD_DIM]
            s = lax.dot_general(q, kk, NT_DIMS, preferred_element_type=F32) * (HEAD_DIM ** -0.5)
            s = s - slope * distf
            s = jnp.where(valid, s, -jnp.inf)
            sink = sink_ref[h]
            m = jnp.maximum(jnp.max(s, axis=-1, keepdims=True), sink)
            p = jnp.exp(s - m)
            denom = jnp.sum(p, axis=-1, keepdims=True) + jnp.exp(sink - m)
            o = jnp.dot(p.astype(BF16), vv, preferred_element_type=F32)
            o_scr[:, h * HEAD_DIM:(h + 1) * HEAD_DIM] = o / denom
    a = o_scr[...]
    ms = jnp.mean(a * a, axis=-1, keepdims=True)
    o_ref[0] = (a * lax.rsqrt(ms + EPS) * g_ref[...]).astype(o_ref.dtype)


def _attention(proj3, sinks, g_attn_out, n_heads, kv_w):
    b, seq, _ = proj3.shape
    attn_w = n_heads * HEAD_DIM
    n_kv = kv_w // HEAD_DIM
    tq = WINDOW
    span = min(3 * WINDOW, seq)
    kcol = attn_w // kv_w
    return pl.pallas_call(
        functools.partial(_attn_kernel, n_heads=n_heads, n_kv=n_kv, tq=tq, span=span, seq=seq),
        grid=(b, seq // tq),
        in_specs=[
            pl.BlockSpec((1, tq, attn_w), lambda bi, i: (bi, i, 0)),
            pl.BlockSpec((1, seq, kv_w), lambda bi, i: (bi, 0, kcol)),
            pl.BlockSpec((1, seq, kv_w), lambda bi, i: (bi, 0, kcol + 1)),
            pl.BlockSpec(memory_space=pltpu.SMEM),
            pl.BlockSpec((1, attn_w), lambda bi, i: (0, 0)),
        ],
        out_specs=pl.BlockSpec((1, tq, attn_w), lambda bi, i: (bi, i, 0)),
        out_shape=jax.ShapeDtypeStruct((b, seq, attn_w), BF16),
        scratch_shapes=[pltpu.VMEM((tq, attn_w), F32)],
        compiler_params=_cparams(("arbitrary", "arbitrary")),
        name="attention",
    )(proj3, proj3, proj3, sinks.astype(F32), g_attn_out.reshape(1, attn_w))


def _fft_plan(seq):
    n = 2 * seq
    n2 = 1 << (int(math.log2(n)) // 2)
    n1 = n // n2
    return n, n1, n2


@functools.lru_cache(maxsize=None)
def _dft_constants(seq):
    n, n1, n2 = _fft_plan(seq)
    n1h = n1 // 2

    def blockc(re, im):
        return np.block([[re, -im], [im, re]])

    k1 = np.arange(n1)[:, None]
    a1 = 2.0 * np.pi * (k1 * np.arange(n1h)[None, :]) / n1
    f1 = blockc(np.cos(a1), -np.sin(a1))
    f1i = blockc(np.cos(a1.T), np.sin(a1.T))
    kk1 = np.arange(n1)[:, None, None]
    kk2 = np.arange(n2)[None, :, None]
    nn2 = np.arange(n2)[None, None, :]
    a2 = 2.0 * np.pi * (nn2 * (kk1 + n1 * kk2)) / n
    c2, s2 = np.cos(a2), np.sin(a2)
    g2 = np.concatenate([np.concatenate([c2, s2], axis=2),
                         np.concatenate([-s2, c2], axis=2)], axis=1)
    c2t, s2t = np.swapaxes(c2, 1, 2), np.swapaxes(s2, 1, 2)
    g2i = np.concatenate([np.concatenate([c2t, -s2t], axis=2),
                          np.concatenate([s2t, c2t], axis=2)], axis=1)
    return (f1.astype(np.float32), f1i.astype(np.float32),
            g2.astype(np.float32), g2i.astype(np.float32))


def _pack_complex(re, im):
    rb = lax.bitcast_convert_type(re.astype(BF16).astype(F32), jnp.uint32)
    ib = lax.bitcast_convert_type(im.astype(BF16).astype(F32), jnp.uint32)
    return (rb & jnp.uint32(0xFFFF0000)) | (ib >> 16)


def _unpack_complex(w):
    re = lax.bitcast_convert_type(w & jnp.uint32(0xFFFF0000), F32).astype(BF16)
    im = lax.bitcast_convert_type(w << 16, F32).astype(BF16)
    return re, im


def _fft_stage1(load_slab, f1_ref, a_scr, n1, n2):
    def body(s, carry):
        a = jnp.dot(f1_ref[...], load_slab(s), preferred_element_type=F32)
        a_scr[pl.ds(pl.multiple_of(s * _pitch(n1), SUBLANES), n1), :] = _pack_complex(a[:n1], a[n1:])
        return carry
    lax.fori_loop(0, n2, body, 0, unroll=FFT_UNROLL)


def _pitch(n1):
    return n1 + SUBLANES


def _load_freq_rows(a_scr, k1, n1, n2):
    re, im = _unpack_complex(a_scr[pl.ds(k1, n2, stride=_pitch(n1)), :])
    return jnp.concatenate([re, im], axis=0)


def _shortconv_kernel(p_ref, w_ref, b_ref, o_ref, u_scr, *, seq, n2, chunk):
    n1h = seq // n2
    up = _pitch(n2)
    w0, w1, w2 = w_ref[0:1, :], w_ref[1:2, :], w_ref[2:3, :]
    bias = b_ref[...]
    halo = 16
    for c in range(seq // chunk):
        r0 = c * chunk
        lo = max(r0 - halo, 0)
        hi = min(r0 + chunk + halo, seq)
        win = p_ref[0, lo:hi, :].astype(F32)
        off = r0 - lo
        cur = win[off:off + chunk]
        if r0 > 0:
            prev = win[off - 1:off - 1 + chunk]
        else:
            prev = jnp.concatenate([jnp.zeros((1, LANES), F32), win[0:chunk - 1]], axis=0)
        if r0 + chunk < seq:
            nxt = win[off + 1:off + 1 + chunk]
        else:
            nxt = jnp.concatenate([win[off + 1:off + chunk], jnp.zeros((1, LANES), F32)], axis=0)
        u = prev * w0 + cur * w1 + nxt * w2 + bias
        for gi in range(chunk // n2):
            r1 = (r0 // n2 + gi) * up
            u_scr[r1:r1 + n2, :] = u[gi * n2:(gi + 1) * n2]

    def body(s, carry):
        o_ref[0, s] = u_scr[pl.ds(s, n1h, stride=up), :].astype(o_ref.dtype)
        return carry
    lax.fori_loop(0, n2, body, 0, unroll=FFT_UNROLL)


def _shortconv(proj3, w_short, b_short, col0):
    b, seq, _ = proj3.shape
    cu = w_short.shape[1]
    _, n1, n2 = _fft_plan(seq)
    n1h = n1 // 2
    chunk = min(512, seq)
    cb0 = col0 // LANES
    return pl.pallas_call(
        functools.partial(_shortconv_kernel, seq=seq, n2=n2, chunk=chunk),
        grid=(b, cu // LANES),
        in_specs=[
            pl.BlockSpec((1, seq, LANES), lambda bi, c: (bi, 0, cb0 + c)),
            pl.BlockSpec((3, LANES), lambda bi, c: (0, c)),
            pl.BlockSpec((1, LANES), lambda bi, c: (0, c)),
        ],
        out_specs=pl.BlockSpec((1, n2, n1h, LANES), lambda bi, c: (bi, 0, 0, c)),
        out_shape=jax.ShapeDtypeStruct((b, n2, n1h, cu), BF16),
        scratch_shapes=[pltpu.VMEM((n1h * _pitch(n2), LANES), F32)],
        compiler_params=_cparams(("arbitrary", "arbitrary")),
        name="shortconv",
    )(proj3, w_short, b_short.reshape(1, cu))


def _filtergen_kernel(z_ref, w1_ref, b1_ref, f1_ref, w2_ref, b2_ref, f2_ref, w3_ref, b3_ref, f3_ref,
                      w4_ref, dl_ref, o_ref, h_scr):
    j = pl.program_id(1)

    @pl.when(j == 0)
    def _():
        z = z_ref[...]
        h = jnp.sin(f1_ref[...] * (_dot3(z, w1_ref[...]) + b1_ref[...]))
        h = jnp.sin(f2_ref[...] * (_dot3(h, w2_ref[...]) + b2_ref[...]))
        h_scr[...] = jnp.sin(f3_ref[...] * (_dot3(h, w3_ref[...]) + b3_ref[...]))

    t01 = z_ref[:, 0:1]
    decay = jnp.exp(-t01 * dl_ref[...])
    o_ref[...] = _dot3(h_scr[...], w4_ref[...]) * decay


@functools.lru_cache(maxsize=None)
def _filter_features(seq):
    _, n1, n2 = _fft_plan(seq)
    n1h = n1 // 2
    t01 = np.linspace(0.0, 1.0, seq, dtype=np.float32)[:, None]
    ang = (2.0 * math.pi * np.arange(seq, dtype=np.float32)[:, None] / seq).astype(np.float32)
    bands = np.linspace(1e-4, POS_BANDS - 1, POS_BANDS, dtype=np.float32)[None, :]
    arg = (bands * ang).astype(np.float32)
    z = np.concatenate([t01, np.cos(arg), -np.sin(arg)], axis=-1).astype(np.float32)
    pos = (np.arange(n1h)[None, :] * n2 + np.arange(n2)[:, None]).reshape(-1)
    zp = np.zeros((seq, FEAT_PAD), np.float32)
    zp[:, :z.shape[1]] = z[pos]
    return zp


def _filtergen(seq, w1, b1, f1, w2, b2, f2, w3, b3, f3, w4, hy_ch):
    fw = w1.shape[1]
    fo = w4.shape[1]
    z = jnp.asarray(_filter_features(seq))
    pe = z.shape[1]
    w1 = jnp.pad(w1, ((0, pe - w1.shape[0]), (0, 0)))
    deltas = np.abs(np.linspace(math.log(DECAY_FAST) / DECAY_TARGET, math.log(DECAY_SLOW) / DECAY_TARGET,
                                hy_ch, dtype=np.float32))
    dl = jnp.asarray(np.tile(deltas, fo // hy_ch).reshape(1, fo))
    tr = min(512, seq)
    tc = min(512, fo)
    row = lambda a: a.reshape(1, -1)
    full = lambda shp: pl.BlockSpec(shp, lambda i, j: (0, 0))
    return pl.pallas_call(
        _filtergen_kernel,
        grid=(seq // tr, fo // tc),
        in_specs=[
            pl.BlockSpec((tr, pe), lambda i, j: (i, 0)),
            full((pe, fw)), full((1, fw)), full((1, fw)),
            full((fw, fw)), full((1, fw)), full((1, fw)),
            full((fw, fw)), full((1, fw)), full((1, fw)),
            pl.BlockSpec((fw, tc), lambda i, j: (0, j)),
            pl.BlockSpec((1, tc), lambda i, j: (0, j)),
        ],
        out_specs=pl.BlockSpec((tr, tc), lambda i, j: (i, j)),
        out_shape=jax.ShapeDtypeStruct((seq, fo), F32),
        scratch_shapes=[pltpu.VMEM((tr, fw), F32)],
        compiler_params=_cparams(("arbitrary", "arbitrary")),
        name="filtergen",
    )(z, w1, row(b1), row(f1), w2, row(b2), row(f2), w3, row(b3), row(f3), w4, dl)


def _fftfwd_kernel(h_ref, f1_ref, g2_ref, o_ref, a_scr, *, n1, n2, k1c, scale):
    kc = pl.program_id(1)

    @pl.when(kc == 0)
    def _():
        _fft_stage1(lambda s: h_ref[s].astype(BF16), f1_ref, a_scr, n1, n2)

    def body(kk, carry):
        k1 = kc * k1c + kk
        x = jnp.dot(g2_ref[kk], _load_freq_rows(a_scr, k1, n1, n2), preferred_element_type=F32)
        o_ref[kk] = x * scale
        return carry
    lax.fori_loop(0, k1c, body, 0, unroll=True)


def _fftfwd_real(hs, seq):
    n, n1, n2 = _fft_plan(seq)
    n1h = n1 // 2
    cols = hs.shape[-1]
    f1, _, g2, _ = _dft_constants(seq)
    f1r = jnp.asarray(f1[:, :n1h], BF16)
    g2b = jnp.asarray(g2, BF16)
    k1c = min(8, n1)
    return pl.pallas_call(
        functools.partial(_fftfwd_kernel, n1=n1, n2=n2, k1c=k1c, scale=1.0 / n),
        grid=(cols // LANES, n1 // k1c),
        in_specs=[
            pl.BlockSpec((n2, n1h, LANES), lambda c, k: (0, 0, c)),
            pl.BlockSpec((2 * n1, n1h), lambda c, k: (0, 0)),
            pl.BlockSpec((k1c, 2 * n2, 2 * n2), lambda c, k: (k, 0, 0)),
        ],
        out_specs=pl.BlockSpec((k1c, 2 * n2, LANES), lambda c, k: (k, 0, c)),
        out_shape=jax.ShapeDtypeStruct((n1, 2 * n2, cols), F32),
        scratch_shapes=[pltpu.VMEM((n2 * _pitch(n1), LANES), jnp.uint32)],
        compiler_params=_cparams(("arbitrary", "arbitrary")),
        name="fftfwd",
    )(hs.reshape(n2, n1h, cols), f1r, g2b)


def _hyena_kernel(u_ref, g_ref, skip_ref, hf_ref, hb_ref, f1_ref, f1i_ref, g2_ref, g2i_ref, o_ref, a_scr,
                  *z_scr, n1, n2, k1c):
    kc = pl.program_id(2)
    n1h = n1 // 2
    zp = _pitch(n1h)

    @pl.when(kc == 0)
    def _():
        _fft_stage1(lambda s: jnp.concatenate([u_ref[0, s], u_ref[1, s]], axis=0), f1_ref, a_scr, n1, n2)

    def body(kk, carry):
        k1 = kc * k1c + kk
        x = jnp.dot(g2_ref[kk], _load_freq_rows(a_scr, k1, n1, n2), preferred_element_type=F32)
        xr, xi = x[:n2], x[n2:]
        hf = hf_ref[kk]
        hb = hb_ref[kk]
        hr = hf[:n2] + hb[:n2]
        hi = hf[n2:] - hb[n2:]
        y = jnp.concatenate([xr * hr - xi * hi, xr * hi + xi * hr], axis=0).astype(BF16)
        bq = jnp.dot(g2i_ref[kk], y, preferred_element_type=F32)
        a_scr[pl.ds(k1, n2, stride=_pitch(n1)), :] = _pack_complex(bq[:n2], bq[n2:])
        return carry
    lax.fori_loop(0, k1c, body, 0, unroll=True)

    @pl.when(kc == pl.num_programs(2) - 1)
    def _():
        skip = skip_ref[...]

        def body3(s, carry):
            re, im = _unpack_complex(a_scr[pl.ds(pl.multiple_of(s * _pitch(n1), SUBLANES), n1), :])
            y = jnp.dot(f1i_ref[...], jnp.concatenate([re, im], axis=0), preferred_element_type=F32)
            for bb in range(2):
                yb = y[bb * n1h:(bb + 1) * n1h]
                z = g_ref[bb, s].astype(F32) * (yb + skip * u_ref[bb, s].astype(F32))
                if z_scr:
                    z_scr[0][bb, pl.ds(pl.multiple_of(s * zp, SUBLANES), n1h), :] = z
                else:
                    o_ref[bb, s] = z.astype(o_ref.dtype)
            return carry
        lax.fori_loop(0, n2, body3, 0, unroll=FFT_UNROLL)

        if z_scr:
            def body4(r, carry):
                for bb in range(2):
                    rows = z_scr[0][bb, pl.ds(r, n2, stride=zp), :]
                    o_ref[bb, pl.ds(pl.multiple_of(r * n2, n2), n2), :] = rows.astype(o_ref.dtype)
                return carry
            lax.fori_loop(0, n1h, body4, 0, unroll=FFT_UNROLL)


def _hyena_conv(u_arr, u_col, g_arr, g_col, skip, tf, order, seq, hy_ch, token_order):
    b = u_arr.shape[0]
    n, n1, n2 = _fft_plan(seq)
    n1h = n1 // 2
    f1, f1i, g2, g2i = _dft_constants(seq)
    k1c = min(8, n1)
    ucb, gcb = u_col // LANES, g_col // LANES
    fcb, bcb = (2 * order) * hy_ch // LANES, (2 * order + 1) * hy_ch // LANES
    scratch = [pltpu.VMEM((n2 * _pitch(n1), LANES), jnp.uint32)]
    if token_order:
        out_spec = pl.BlockSpec((2, seq, LANES), lambda c, p, k: (p, 0, c))
        out_shape = jax.ShapeDtypeStruct((b, seq, hy_ch), BF16)
        scratch.append(pltpu.VMEM((2, n2 * _pitch(n1h), LANES), F32))
    else:
        out_spec = pl.BlockSpec((2, n2, n1h, LANES), lambda c, p, k: (p, 0, 0, c))
        out_shape = jax.ShapeDtypeStruct((b, n2, n1h, hy_ch), BF16)
    return pl.pallas_call(
        functools.partial(_hyena_kernel, n1=n1, n2=n2, k1c=k1c),
        grid=(hy_ch // LANES, b // 2, n1 // k1c),
        in_specs=[
            pl.BlockSpec((2, n2, n1h, LANES), lambda c, p, k: (p, 0, 0, ucb + c)),
            pl.BlockSpec((2, n2, n1h, LANES), lambda c, p, k: (p, 0, 0, gcb + c)),
            pl.BlockSpec((1, LANES), lambda c, p, k: (0, c)),
            pl.BlockSpec((k1c, 2 * n2, LANES), lambda c, p, k: (k, 0, fcb + c)),
            pl.BlockSpec((k1c, 2 * n2, LANES), lambda c, p, k: (k, 0, bcb + c)),
            pl.BlockSpec((2 * n1, n1), lambda c, p, k: (0, 0)),
            pl.BlockSpec((n1, 2 * n1), lambda c, p, k: (0, 0)),
            pl.BlockSpec((k1c, 2 * n2, 2 * n2), lambda c, p, k: (k, 0, 0)),
            pl.BlockSpec((k1c, 2 * n2, 2 * n2), lambda c, p, k: (k, 0, 0)),
        ],
        out_specs=out_spec,
        out_shape=out_shape,
        scratch_shapes=scratch,
        compiler_params=_cparams(("arbitrary", "arbitrary", "arbitrary")),
        name="hyenaconv",
    )(u_arr, g_arr, skip.reshape(1, hy_ch), tf, tf,
      jnp.asarray(f1, BF16), jnp.asarray(f1i, BF16), jnp.asarray(g2, BF16), jnp.asarray(g2i, BF16))


def _outproj_kernel(a_ref, z_ref, x_ref, wa_ref, wh_ref, gh_ref, gf_ref, wrh_ref, wrl_ref, br_ref, c0_ref,
                    tri_ref, x1_ref, xn_ref, idx_ref, gate_ref, rank_ref, cnt_ref, run_scr, *, n_exp):
    i = pl.program_id(0)

    @pl.when(i == 0)
    def _():
        run_scr[...] = c0_ref[...]

    z = z_ref[...].astype(F32)
    zn = (z * lax.rsqrt(jnp.mean(z * z, axis=-1, keepdims=True) + EPS) * gh_ref[...]).astype(BF16)
    mixed = (jnp.dot(a_ref[...], wa_ref[...], preferred_element_type=F32)
             + jnp.dot(zn, wh_ref[...], preferred_element_type=F32))
    x1 = x_ref[...] + mixed
    x1_ref[...] = x1
    xn = x1 * lax.rsqrt(jnp.mean(x1 * x1, axis=-1, keepdims=True) + EPS) * gf_ref[...]
    half = xn.shape[1] // 2
    packed = _pack_complex(xn[:, :half], xn[:, half:])
    for c in range(SUBLANES):
        xn_ref[pl.ds(c, xn.shape[0], stride=SUBLANES), :] = packed[:, c * LANES:(c + 1) * LANES]

    xh, xl = _split_bf16(xn)
    dg = functools.partial(lax.dot_general, dimension_numbers=NT_DIMS, preferred_element_type=F32)
    logit = dg(wrh_ref[...], xh) + dg(wrh_ref[...], xl) + dg(wrl_ref[...], xh) + br_ref[...]
    tm = logit.shape[1]
    eio = lax.broadcasted_iota(jnp.int32, (n_exp, tm), 0)
    vals, onehots = [], []
    cur = logit
    for k in range(TOP_K):
        m = jnp.max(cur, axis=0, keepdims=True)
        idx = jnp.min(jnp.where(cur == m, eio, n_exp), axis=0, keepdims=True)
        oh = eio == idx
        vals.append(m)
        onehots.append(oh)
        idx_ref[k:k + 1, :] = idx
        cur = jnp.where(oh, -jnp.inf, cur)
    ex = [jnp.exp(v - vals[0]) for v in vals]
    tot = ex[0] + ex[1] + ex[2] + ex[3]
    for k in range(TOP_K):
        gate_ref[k:k + 1, :] = ex[k] / tot
    ohsum = jnp.zeros((n_exp, tm), F32)
    for oh in onehots:
        ohsum = ohsum + oh.astype(F32)
    base = jnp.dot(ohsum.astype(BF16), tri_ref[...], preferred_element_type=F32) + run_scr[...]
    for k in range(TOP_K):
        rank_ref[k:k + 1, :] = jnp.sum(jnp.where(onehots[k], base, 0.0), axis=0, keepdims=True).astype(jnp.int32)
    run_scr[...] = run_scr[...] + jnp.sum(ohsum, axis=1, keepdims=True)
    cnt_ref[...] = run_scr[...]


def _outproj_router(attn_n, z2, x2d, wa, wh, g_hy, g_ffn, wr_hi, wr_lo, b_router, count0):
    t, d = x2d.shape
    aw, hc = attn_n.shape[1], z2.shape[1]
    n_exp = wr_hi.shape[0]
    tm = min(256, t)
    tri = jnp.asarray(np.triu(np.ones((tm, tm), np.float32), k=1), BF16)
    full = lambda shp: pl.BlockSpec(shp, lambda i: (0, 0))
    rowblk = lambda w: pl.BlockSpec((tm, w), lambda i: (i, 0))
    colblk = pl.BlockSpec((TOP_K, tm), lambda i: (0, i))
    return pl.pallas_call(
        functools.partial(_outproj_kernel, n_exp=n_exp),
        grid=(t // tm,),
        in_specs=[rowblk(aw), rowblk(hc), rowblk(d), full((aw, d)), full((hc, d)), full((1, hc)), full((1, d)),
                  full((n_exp, d)), full((n_exp, d)), full((n_exp, 1)), full((n_exp, 1)), full((tm, tm))],
        out_specs=[rowblk(d), pl.BlockSpec((tm * SUBLANES, LANES), lambda i: (i, 0)),
                   colblk, colblk, colblk, full((n_exp, 1))],
        out_shape=[jax.ShapeDtypeStruct((t, d), F32), jax.ShapeDtypeStruct((t * SUBLANES, LANES), jnp.uint32),
                   jax.ShapeDtypeStruct((TOP_K, t), jnp.int32), jax.ShapeDtypeStruct((TOP_K, t), F32),
                   jax.ShapeDtypeStruct((TOP_K, t), jnp.int32), jax.ShapeDtypeStruct((n_exp, 1), F32)],
        scratch_shapes=[pltpu.VMEM((n_exp, 1), F32)],
        compiler_params=_cparams(("arbitrary",)),
        name="outproj_router",
    )(attn_n, z2, x2d, wa, wh, g_hy.reshape(1, hc), g_ffn.reshape(1, d), wr_hi, wr_lo,
      b_router.reshape(n_exp, 1), count0, tri)


def _dest_kernel(ps_ref, idx_ref, rank_ref, o_ref, *, n_exp):
    idx = idx_ref[...]
    acc = rank_ref[...]
    for e in range(n_exp):
        acc = acc + jnp.where(idx == e, ps_ref[e], 0)
    o_ref[...] = acc


def _dest_rows(pstart, idx, rank):
    k, t = idx.shape
    tb = min(2048, t)
    blk = pl.BlockSpec((k, tb), lambda i: (0, i))
    return pl.pallas_call(
        functools.partial(_dest_kernel, n_exp=pstart.shape[0]),
        grid=(t // tb,),
        in_specs=[pl.BlockSpec(memory_space=pltpu.SMEM), blk, blk],
        out_specs=blk,
        out_shape=jax.ShapeDtypeStruct((k, t), jnp.int32),
        compiler_params=_cparams(("arbitrary",)),
        name="dest",
    )(pstart, idx, rank)


def _dispatch_kernel(dest_ref, xn_ref, xs_in_ref, xs_ref, sem):
    del xs_in_ref
    tm = xn_ref.shape[0] // SUBLANES

    def row_copy(t, d):
        src = xn_ref.at[pl.ds(pl.multiple_of(t * SUBLANES, SUBLANES), SUBLANES)]
        return pltpu.make_async_copy(src, xs_ref.at[d], sem)

    def issue(t, carry):
        for k in range(TOP_K):
            row_copy(t, dest_ref[k, t]).start(priority=k % 2)
        return carry
    lax.fori_loop(0, tm, issue, 0)

    def drain(t, carry):
        for k in range(TOP_K):
            row_copy(0, 0).wait()
        return carry
    lax.fori_loop(0, tm, drain, 0)


def _dispatch(dest, xn, xs):
    t = dest.shape[1]
    tm = min(256, t)
    return pl.pallas_call(
        _dispatch_kernel,
        grid=(t // tm,),
        in_specs=[pl.BlockSpec((TOP_K, tm), lambda i: (0, i), memory_space=pltpu.SMEM),
                  pl.BlockSpec((tm * SUBLANES, LANES), lambda i: (i, 0)),
                  pl.BlockSpec(memory_space=pl.ANY)],
        out_specs=pl.BlockSpec(memory_space=pl.ANY),
        out_shape=jax.ShapeDtypeStruct(xs.shape, xs.dtype),
        scratch_shapes=[pltpu.SemaphoreType.DMA(())],
        input_output_aliases={2: 0},
        compiler_params=_cparams(("arbitrary",)),
        name="dispatch",
    )(dest, xn, xs)


def _expert_kernel(be_ref, nu_ref, x_ref, wg_ref, wu_ref, bg_ref, bu_ref, wd_ref, bd_ref, y_ref, x_scr, acc_scr):
    del be_ref
    i, j = pl.program_id(0), pl.program_id(1)
    bm, d = x_scr.shape
    half = d // 2

    @pl.when(i < nu_ref[0])
    def _():
        @pl.when(j == 0)
        def _():
            for c in range(SUBLANES):
                hi, lo = _unpack_complex(x_ref[pl.ds(c, bm, stride=SUBLANES), :])
                x_scr[:, c * LANES:(c + 1) * LANES] = hi
                x_scr[:, half + c * LANES:half + (c + 1) * LANES] = lo

        x = x_scr[...]
        g = jnp.dot(x, wg_ref[0], preferred_element_type=F32) + bg_ref[0]
        u = jnp.dot(x, wu_ref[0], preferred_element_type=F32) + bu_ref[0]
        glu = jnp.minimum(g, SWIGLU_LIMIT)
        lin = jnp.clip(u, -SWIGLU_LIMIT, SWIGLU_LIMIT)
        h = glu * jax.nn.sigmoid(SWIGLU_ALPHA * glu) * (lin + 1.0)
        part = jnp.dot(h.astype(BF16), wd_ref[0], preferred_element_type=F32)

        @pl.when(j == 0)
        def _():
            acc_scr[...] = part + bd_ref[0]

        @pl.when(j > 0)
        def _():
            acc_scr[...] = acc_scr[...] + part

        @pl.when(j == pl.num_programs(1) - 1)
        def _():
            for c in range(SUBLANES):
                y_ref[pl.ds(c, bm, stride=SUBLANES), :] = _pack_complex(
                    acc_scr[:, c * LANES:(c + 1) * LANES], acc_scr[:, half + c * LANES:half + (c + 1) * LANES])


def _experts(block_expert, n_used, xs, w_gu, b_gu, w_dn, b_dn):
    n_exp, d, f2 = w_gu.shape
    n_rows = xs.shape[0] // SUBLANES
    f = f2 // 2
    tf = 512 if f % 512 == 0 else 256
    assert f % tf == 0
    nf = f // tf
    bm = MOE_ROWS
    last = lambda i, nu: jnp.minimum(i, nu[0] - 1)
    fcol = lambda i, j, nu: jnp.where(i < nu[0], j, nf - 1)
    grid_spec = pltpu.PrefetchScalarGridSpec(
        num_scalar_prefetch=2,
        grid=(n_rows // bm, nf),
        in_specs=[
            pl.BlockSpec((bm * SUBLANES, LANES), lambda i, j, be, nu: (last(i, nu), 0)),
            pl.BlockSpec((1, d, tf), lambda i, j, be, nu: (be[i], 0, fcol(i, j, nu))),
            pl.BlockSpec((1, d, tf), lambda i, j, be, nu: (be[i], 0, nf + fcol(i, j, nu))),
            pl.BlockSpec((1, 1, tf), lambda i, j, be, nu: (be[i], 0, fcol(i, j, nu))),
            pl.BlockSpec((1, 1, tf), lambda i, j, be, nu: (be[i], 0, nf + fcol(i, j, nu))),
            pl.BlockSpec((1, tf, d), lambda i, j, be, nu: (be[i], fcol(i, j, nu), 0)),
            pl.BlockSpec((1, 1, d), lambda i, j, be, nu: (be[i], 0, 0)),
        ],
        out_specs=pl.BlockSpec((bm * SUBLANES, LANES), lambda i, j, be, nu: (last(i, nu), 0)),
        scratch_shapes=[pltpu.VMEM((bm, d), BF16), pltpu.VMEM((bm, d), F32)],
    )
    return pl.pallas_call(
        _expert_kernel,
        grid_spec=grid_spec,
        out_shape=jax.ShapeDtypeStruct((n_rows * SUBLANES, LANES), jnp.uint32),
        compiler_params=_cparams(("arbitrary", "arbitrary")),
        name="experts",
    )(block_expert, n_used, xs, w_gu, w_gu, b_gu.reshape(n_exp, 1, f2), b_gu.reshape(n_exp, 1, f2),
      w_dn, b_dn.reshape(n_exp, 1, d))


def _combine_kernel(dest_ref, x1_ref, gate_ref, ys_ref, o_ref, buf, sem):
    tm, d = x1_ref.shape
    half = d // 2

    def row_copy(t, k, d_row):
        dst = buf.at[k, pl.ds(pl.multiple_of(t * SUBLANES, SUBLANES), SUBLANES)]
        return pltpu.make_async_copy(ys_ref.at[d_row], dst, sem)

    def issue(t, carry):
        for k in range(TOP_K):
            row_copy(t, k, dest_ref[k, t]).start(priority=k % 2)
        return carry
    lax.fori_loop(0, tm, issue, 0)

    def drain(t, carry):
        for k in range(TOP_K):
            row_copy(0, 0, 0).wait()
        return carry
    lax.fori_loop(0, tm, drain, 0)

    gate = gate_ref[...]
    for c in range(SUBLANES):
        lo_cols = slice(c * LANES, (c + 1) * LANES)
        hi_cols = slice(half + c * LANES, half + (c + 1) * LANES)
        moe_a = x1_ref[:, lo_cols]
        moe_b = x1_ref[:, hi_cols]
        for k in range(TOP_K):
            w = buf[k, pl.ds(c, tm, stride=SUBLANES), :]
            g = gate[:, k:k + 1]
            moe_a = moe_a + g * lax.bitcast_convert_type(w & jnp.uint32(0xFFFF0000), F32)
            moe_b = moe_b + g * lax.bitcast_convert_type(w << 16, F32)
        o_ref[:, lo_cols] = moe_a
        o_ref[:, hi_cols] = moe_b


def _combine(dest, x1, gate_tk, ys):
    t, d = x1.shape
    tm = min(256, t)
    return pl.pallas_call(
        _combine_kernel,
        grid=(t // tm,),
        in_specs=[pl.BlockSpec((TOP_K, tm), lambda i: (0, i), memory_space=pltpu.SMEM),
                  pl.BlockSpec((tm, d), lambda i: (i, 0)),
                  pl.BlockSpec((tm, TOP_K), lambda i: (i, 0)),
                  pl.BlockSpec(memory_space=pl.ANY)],
        out_specs=pl.BlockSpec((tm, d), lambda i: (i, 0)),
        out_shape=jax.ShapeDtypeStruct((t, d), F32),
        scratch_shapes=[pltpu.VMEM((TOP_K, tm * SUBLANES, LANES), jnp.uint32), pltpu.SemaphoreType.DMA(())],
        compiler_params=_cparams(("arbitrary",)),
        name="combine",
    )(dest, x1, gate_tk, ys)


def _mixer(x3, g_mix, w_in_bf, g_q, g_k, sinks, w_short, b_short, flt, hyena_skip, g_attn_out, dims):
    n_heads, kv_w, hy_ch = dims
    b, seq, d = x3.shape
    attn_w = n_heads * HEAD_DIM
    proj = _inproj(x3.reshape(b * seq, d), g_mix, w_in_bf, g_q, g_k, attn_w, kv_w)
    proj3 = proj.reshape(b, seq, proj.shape[1])
    attn_n = _attention(proj3, sinks, g_attn_out, n_heads, kv_w)
    u = _shortconv(proj3, w_short, b_short, attn_w + 2 * kv_w)
    hs = _filtergen(seq, *flt, hy_ch)
    tf = _fftfwd_real(hs, seq)
    z1 = _hyena_conv(u, 0, u, hy_ch, hyena_skip[0], tf, 0, seq, hy_ch, token_order=False)
    z2 = _hyena_conv(z1, 0, u, 2 * hy_ch, hyena_skip[1], tf, 1, seq, hy_ch, token_order=True)
    return attn_n.reshape(b * seq, attn_w), z2.reshape(b * seq, hy_ch)


def kernel(x_prompt, x_sample, g_mix, w_in, g_q, g_k, attn_sinks, w_short, b_short, flt_w1, flt_b1, flt_freq1, flt_w2, flt_b2, flt_freq2, flt_w3, flt_b3, flt_freq3, flt_w4, hyena_skip, g_attn_out, g_hyena_out, w_out, g_ffn, w_router, b_router, w_gate_up, b_gate_up, w_down, b_down):
    assert g_mix.shape[0] == 1, "single-layer trunk"
    d = x_prompt.shape[-1]
    assert d == 2 * SUBLANES * LANES, "routed-row tile layout is built for d_model = 2048"
    n_heads = attn_sinks.shape[-1]
    hy_ch = hyena_skip.shape[-1]
    attn_w = n_heads * HEAD_DIM
    kv_w = (w_in.shape[-1] - attn_w - 3 * hy_ch) // 2
    n_exp = w_router.shape[-1]
    dims = (n_heads, kv_w, hy_ch)

    w_in_bf = w_in[0].astype(BF16)
    wa = w_out[0, :attn_w].astype(BF16)
    wh = w_out[0, attn_w:].astype(BF16)
    wr_t = w_router[0].T
    wr_hi = wr_t.astype(BF16)
    wr_lo = (wr_t - wr_hi.astype(F32)).astype(BF16)
    w_gu = w_gate_up[0].astype(BF16)
    w_dn = w_down[0].astype(BF16)
    flt = (flt_w1[0], flt_b1[0], flt_freq1[0], flt_w2[0], flt_b2[0], flt_freq2[0],
           flt_w3[0], flt_b3[0], flt_freq3[0], flt_w4[0])

    xs_in = (x_prompt, x_sample)
    routed = []
    counts = jnp.zeros((n_exp, 1), F32)
    for x3 in xs_in:
        attn_n, z2 = _mixer(x3, g_mix[0], w_in_bf, g_q[0], g_k[0], attn_sinks[0], w_short[0], b_short[0],
                            flt, hyena_skip[0], g_attn_out[0], dims)
        x1, xn, idx, gate, rank, counts = _outproj_router(
            attn_n, z2, x3.reshape(-1, d), wa, wh, g_hyena_out[0], g_ffn[0], wr_hi, wr_lo, b_router[0], counts)
        routed.append((x1, xn, idx, gate, rank))

    n_slots = sum(r[0].shape[0] for r in routed) * TOP_K
    n_rows = (n_slots // MOE_ROWS + n_exp) * MOE_ROWS
    cnt = counts[:, 0].astype(jnp.int32)
    padded = (cnt + MOE_ROWS - 1) // MOE_ROWS * MOE_ROWS
    pend = jnp.cumsum(padded)
    pstart = (pend - padded).astype(jnp.int32)
    n_blocks = n_rows // MOE_ROWS
    block_row0 = jnp.arange(n_blocks, dtype=jnp.int32) * MOE_ROWS
    block_expert = jnp.minimum(jnp.sum(pend[None, :] <= block_row0[:, None], axis=1), n_exp - 1).astype(jnp.int32)
    n_used = (pend[-1:] // MOE_ROWS).astype(jnp.int32)

    xs = jnp.zeros((n_rows, SUBLANES, LANES), jnp.uint32)
    dests = []
    for (x1, xn, idx, gate, rank) in routed:
        dest = _dest_rows(pstart, idx, rank)
        xs = _dispatch(dest, xn, xs)
        dests.append(dest)
    ys = _experts(block_expert, n_used, xs.reshape(n_rows * SUBLANES, LANES), w_gu, b_gate_up[0], w_dn, b_down[0])
    ys = ys.reshape(n_rows, SUBLANES, LANES)
    outs = []
    for (x1, xn, idx, gate, rank), dest, x3 in zip(routed, dests, xs_in):
        outs.append(_combine(dest, x1, gate.T, ys).reshape(x3.shape))
    return tuple(outs)
```

```python
import functools
import math

import numpy as np
import jax
import jax.numpy as jnp
from jax import lax
from jax.experimental import pallas as pl
from jax.experimental.pallas import tpu as pltpu

HEAD_DIM = 64
WINDOW = 128
TOP_K = 4
EPS = 1e-6
POS_BANDS = 16
FEAT_PAD = 64
DECAY_FAST = 0.3
DECAY_SLOW = 1.5
DECAY_TARGET = 1e-2
SWIGLU_ALPHA = 1.702
SWIGLU_LIMIT = 7.0

LANES = 128
SUBLANES = 8
VMEM_LIMIT = 56 * 1024 * 1024
MOE_ROWS = 512
FFT_UNROLL = 4
F32 = jnp.float32
BF16 = jnp.bfloat16
NT_DIMS = (((1,), (1,)), ((), ()))


def _cparams(sem):
    return pltpu.CompilerParams(dimension_semantics=sem, vmem_limit_bytes=VMEM_LIMIT)


def _split_bf16(a):
    hi = a.astype(BF16)
    lo = (a - hi.astype(F32)).astype(BF16)
    return hi, lo


def _dot3(a, b):
    ah, al = _split_bf16(a)
    bh, bl = _split_bf16(b)
    d = functools.partial(jnp.dot, preferred_element_type=F32)
    return d(ah, bh) + d(ah, bl) + d(al, bh)


def _inproj_kernel(x_ref, g_ref, w_ref, gq_ref, gk_ref, s_ref, o_ref, h_scr, *, n_q_tiles):
    j = pl.program_id(1)

    @pl.when(j == 0)
    def _():
        x = x_ref[...]
        ms = jnp.mean(x * x, axis=-1, keepdims=True)
        h_scr[...] = (x * lax.rsqrt(ms + EPS) * g_ref[...]).astype(BF16)

    acc = jnp.dot(h_scr[...], w_ref[...], preferred_element_type=F32)
    tn = acc.shape[1]

    def head_norm(a, g):
        hi, lo = _split_bf16(a * a)
        s = s_ref[:a.shape[1], :a.shape[1]]
        ss = (jnp.dot(hi, s, preferred_element_type=F32)
              + jnp.dot(lo, s, preferred_element_type=F32))
        return a * lax.rsqrt(ss * (1.0 / HEAD_DIM) + EPS) * g

    @pl.when(j < n_q_tiles)
    def _():
        o_ref[...] = head_norm(acc, gq_ref[...]).astype(o_ref.dtype)

    @pl.when(j == n_q_tiles)
    def _():
        half = tn // 2
        o_ref[:, :half] = head_norm(acc[:, :half], gk_ref[...]).astype(o_ref.dtype)
        o_ref[:, half:] = acc[:, half:].astype(o_ref.dtype)

    @pl.when(j > n_q_tiles)
    def _():
        o_ref[...] = acc.astype(o_ref.dtype)


def _inproj(x2d, g_mix, w_in_bf, g_q, g_k, attn_w, kv_w):
    t, d = x2d.shape
    in_w = w_in_bf.shape[1]
    tn = 2 * kv_w
    tm = min(1024, t)
    assert attn_w % tn == 0 and in_w % tn == 0 and t % tm == 0
    n_q_tiles = attn_w // tn
    heads_per_tile = tn // HEAD_DIM
    blk = np.kron(np.eye(heads_per_tile, dtype=np.float32), np.ones((HEAD_DIM, HEAD_DIM), np.float32))
    gq_t = jnp.tile(g_q.reshape(1, HEAD_DIM), (1, heads_per_tile))
    gk_t = jnp.tile(g_k.reshape(1, HEAD_DIM), (1, heads_per_tile // 2))
    return pl.pallas_call(
        functools.partial(_inproj_kernel, n_q_tiles=n_q_tiles),
        grid=(t // tm, in_w // tn),
        in_specs=[
            pl.BlockSpec((tm, d), lambda i, j: (i, 0)),
            pl.BlockSpec((1, d), lambda i, j: (0, 0)),
            pl.BlockSpec((d, tn), lambda i, j: (0, j)),
            pl.BlockSpec((1, tn), lambda i, j: (0, 0)),
            pl.BlockSpec((1, tn // 2), lambda i, j: (0, 0)),
            pl.BlockSpec((tn, tn), lambda i, j: (0, 0)),
        ],
        out_specs=pl.BlockSpec((tm, tn), lambda i, j: (i, j)),
        out_shape=jax.ShapeDtypeStruct((t, in_w), BF16),
        scratch_shapes=[pltpu.VMEM((tm, d), BF16)],
        compiler_params=_cparams(("arbitrary", "arbitrary")),
        name="inproj",
    )(x2d, g_mix.reshape(1, d), w_in_bf, gq_t, gk_t, jnp.asarray(blk, BF16))


def _kv_start(i, tq, span, seq):
    return jnp.clip(i * tq - WINDOW, 0, seq - span)


def _attn_kernel(q_ref, k_ref, v_ref, bias_ref, sink_ref, g_ref, o_ref, o_scr, s_scr, p_scr,
                 *, n_heads, n_kv, tq, span, seq):
    i = pl.program_id(1)
    grp = n_heads // n_kv
    rows = 2 * SUBLANES
    start = pl.multiple_of(_kv_start(i, tq, span, seq), WINDOW)
    for kh in range(n_kv):
        kk = k_ref[0, pl.ds(start, span), kh * HEAD_DIM:(kh + 1) * HEAD_DIM] * (HEAD_DIM ** -0.5)
        for h in range(kh * grp, (kh + 1) * grp):
            q = q_ref[0, :, h * HEAD_DIM:(h + 1) * HEAD_DIM]
            s_scr[h * tq:(h + 1) * tq, :] = (lax.dot_general(q, kk, NT_DIMS, preferred_element_type=F32)
                                             + bias_ref[0, h * tq:(h + 1) * tq, :])
    for h in range(n_heads):
        sink = sink_ref[h]
        for r in range(h * tq, (h + 1) * tq, rows):
            s = s_scr[r:r + rows, :]
            m = jnp.maximum(jnp.max(s, axis=-1, keepdims=True), sink)
            p = jnp.exp(s - m)
            denom = jnp.sum(p, axis=-1, keepdims=True) + jnp.exp(sink - m)
            p_scr[r:r + rows, :] = (p / denom).astype(BF16)
    for kh in range(n_kv):
        vv = v_ref[0, pl.ds(start, span), kh * HEAD_DIM:(kh + 1) * HEAD_DIM]
        for h in range(kh * grp, (kh + 1) * grp):
            o_scr[:, h * HEAD_DIM:(h + 1) * HEAD_DIM] = jnp.dot(p_scr[h * tq:(h + 1) * tq, :], vv,
                                                                preferred_element_type=F32)
    a = o_scr[...]
    ms = jnp.mean(a * a, axis=-1, keepdims=True)
    o_ref[0] = (a * lax.rsqrt(ms + EPS) * g_ref[...]).astype(o_ref.dtype)


@functools.lru_cache(maxsize=None)
def _alibi_window_bias(n_heads, tq, span):
    slopes = np.exp2(-8.0 * np.arange(1, n_heads + 1, dtype=np.float32) / n_heads).astype(np.float32)
    r = np.arange(tq)[:, None]
    c = np.arange(span)[None, :]
    tables = []
    for v in range(3):
        dist = np.abs(r - c + v * WINDOW)
        bias = np.where(dist <= WINDOW, -slopes[:, None, None] * dist.astype(np.float32), -np.inf)
        tables.append(bias.reshape(n_heads * tq, span))
    return np.stack(tables).astype(np.float32)


def _attention(proj3, sinks, g_attn_out, n_heads, kv_w):
    b, seq, _ = proj3.shape
    attn_w = n_heads * HEAD_DIM
    n_kv = kv_w // HEAD_DIM
    tq = WINDOW
    span = min(3 * WINDOW, seq)
    kcol = attn_w // kv_w
    bias = jnp.asarray(_alibi_window_bias(n_heads, tq, span))
    return pl.pallas_call(
        functools.partial(_attn_kernel, n_heads=n_heads, n_kv=n_kv, tq=tq, span=span, seq=seq),
        grid=(b, seq // tq),
        in_specs=[
            pl.BlockSpec((1, tq, attn_w), lambda bi, i: (bi, i, 0)),
            pl.BlockSpec((1, seq, kv_w), lambda bi, i: (bi, 0, kcol)),
            pl.BlockSpec((1, seq, kv_w), lambda bi, i: (bi, 0, kcol + 1)),
            pl.BlockSpec((1, n_heads * tq, span),
                         lambda bi, i: ((i * tq - _kv_start(i, tq, span, seq)) // WINDOW, 0, 0)),
            pl.BlockSpec(memory_space=pltpu.SMEM),
            pl.BlockSpec((1, attn_w), lambda bi, i: (0, 0)),
        ],
        out_specs=pl.BlockSpec((1, tq, attn_w), lambda bi, i: (bi, i, 0)),
        out_shape=jax.ShapeDtypeStruct((b, seq, attn_w), BF16),
        scratch_shapes=[pltpu.VMEM((tq, attn_w), F32), pltpu.VMEM((n_heads * tq, span), F32),
                        pltpu.VMEM((n_heads * tq, span), BF16)],
        compiler_params=_cparams(("arbitrary", "arbitrary")),
        name="attention",
    )(proj3, proj3, proj3, bias, sinks.astype(F32), g_attn_out.reshape(1, attn_w))


def _fft_plan(seq):
    n = 2 * seq
    n2 = 1 << (int(math.log2(n)) // 2)
    n1 = n // n2
    return n, n1, n2


@functools.lru_cache(maxsize=None)
def _dft_constants(seq):
    n, n1, n2 = _fft_plan(seq)
    n1h = n1 // 2

    def blockc(re, im):
        return np.block([[re, -im], [im, re]])

    k1 = np.arange(n1)[:, None]
    a1 = 2.0 * np.pi * (k1 * np.arange(n1h)[None, :]) / n1
    f1 = blockc(np.cos(a1), -np.sin(a1))
    f1i = blockc(np.cos(a1.T), np.sin(a1.T))
    kk1 = np.arange(n1)[:, None, None]
    kk2 = np.arange(n2)[None, :, None]
    nn2 = np.arange(n2)[None, None, :]
    a2 = 2.0 * np.pi * (nn2 * (kk1 + n1 * kk2)) / n
    c2, s2 = np.cos(a2), np.sin(a2)
    g2 = np.concatenate([np.concatenate([c2, s2], axis=2),
                         np.concatenate([-s2, c2], axis=2)], axis=1)
    c2t, s2t = np.swapaxes(c2, 1, 2), np.swapaxes(s2, 1, 2)
    g2i = np.concatenate([np.concatenate([c2t, -s2t], axis=2),
                          np.concatenate([s2t, c2t], axis=2)], axis=1)
    return (f1.astype(np.float32), f1i.astype(np.float32),
            g2.astype(np.float32), g2i.astype(np.float32))


def _pack_complex(re, im):
    rb = lax.bitcast_convert_type(re.astype(BF16).astype(F32), jnp.uint32)
    ib = lax.bitcast_convert_type(im.astype(BF16).astype(F32), jnp.uint32)
    return (rb & jnp.uint32(0xFFFF0000)) | (ib >> 16)


def _unpack_complex(w):
    re = lax.bitcast_convert_type(w & jnp.uint32(0xFFFF0000), F32).astype(BF16)
    im = lax.bitcast_convert_type(w << 16, F32).astype(BF16)
    return re, im


def _fft_stage1(load_slab, f1_ref, a_scr, n1, n2):
    def body(s, carry):
        a = jnp.dot(f1_ref[...], load_slab(s), preferred_element_type=F32)
        a_scr[pl.ds(pl.multiple_of(s * _pitch(n1), SUBLANES), n1), :] = _pack_complex(a[:n1], a[n1:])
        return carry
    lax.fori_loop(0, n2, body, 0, unroll=FFT_UNROLL)


def _pitch(n1):
    return n1 + SUBLANES


def _load_freq_rows(a_scr, k1, n1, n2):
    re, im = _unpack_complex(a_scr[pl.ds(k1, n2, stride=_pitch(n1)), :])
    return jnp.concatenate([re, im], axis=0)


def _shortconv_kernel(p_ref, w_ref, b_ref, o_ref, u_scr, *, seq, n2, chunk):
    n1h = seq // n2
    up = _pitch(n2)
    w0, w1, w2 = w_ref[0:1, :], w_ref[1:2, :], w_ref[2:3, :]
    bias = b_ref[...]
    halo = 16
    for c in range(seq // chunk):
        r0 = c * chunk
        lo = max(r0 - halo, 0)
        hi = min(r0 + chunk + halo, seq)
        win = p_ref[0, lo:hi, :].astype(F32)
        off = r0 - lo
        cur = win[off:off + chunk]
        if r0 > 0:
            prev = win[off - 1:off - 1 + chunk]
        else:
            prev = jnp.concatenate([jnp.zeros((1, LANES), F32), win[0:chunk - 1]], axis=0)
        if r0 + chunk < seq:
            nxt = win[off + 1:off + 1 + chunk]
        else:
            nxt = jnp.concatenate([win[off + 1:off + chunk], jnp.zeros((1, LANES), F32)], axis=0)
        u = prev * w0 + cur * w1 + nxt * w2 + bias
        for gi in range(chunk // n2):
            r1 = (r0 // n2 + gi) * up
            u_scr[r1:r1 + n2, :] = u[gi * n2:(gi + 1) * n2]

    def body(s, carry):
        o_ref[0, s] = u_scr[pl.ds(s, n1h, stride=up), :].astype(o_ref.dtype)
        return carry
    lax.fori_loop(0, n2, body, 0, unroll=FFT_UNROLL)


def _shortconv(proj3, w_short, b_short, col0):
    b, seq, _ = proj3.shape
    cu = w_short.shape[1]
    _, n1, n2 = _fft_plan(seq)
    n1h = n1 // 2
    chunk = min(512, seq)
    cb0 = col0 // LANES
    return pl.pallas_call(
        functools.partial(_shortconv_kernel, seq=seq, n2=n2, chunk=chunk),
        grid=(b, cu // LANES),
        in_specs=[
            pl.BlockSpec((1, seq, LANES), lambda bi, c: (bi, 0, cb0 + c)),
            pl.BlockSpec((3, LANES), lambda bi, c: (0, c)),
            pl.BlockSpec((1, LANES), lambda bi, c: (0, c)),
        ],
        out_specs=pl.BlockSpec((1, n2, n1h, LANES), lambda bi, c: (bi, 0, 0, c)),
        out_shape=jax.ShapeDtypeStruct((b, n2, n1h, cu), BF16),
        scratch_shapes=[pltpu.VMEM((n1h * _pitch(n2), LANES), F32)],
        compiler_params=_cparams(("arbitrary", "arbitrary")),
        name="shortconv",
    )(proj3, w_short, b_short.reshape(1, cu))


def _filtergen_kernel(z_ref, w1_ref, b1_ref, f1_ref, w2_ref, b2_ref, f2_ref, w3_ref, b3_ref, f3_ref,
                      w4_ref, dl_ref, o_ref, h_scr):
    j = pl.program_id(1)

    @pl.when(j == 0)
    def _():
        z = z_ref[...]
        h = jnp.sin(f1_ref[...] * (_dot3(z, w1_ref[...]) + b1_ref[...]))
        h = jnp.sin(f2_ref[...] * (_dot3(h, w2_ref[...]) + b2_ref[...]))
        h_scr[...] = jnp.sin(f3_ref[...] * (_dot3(h, w3_ref[...]) + b3_ref[...]))

    t01 = z_ref[:, 0:1]
    decay = jnp.exp(-t01 * dl_ref[...])
    o_ref[...] = _dot3(h_scr[...], w4_ref[...]) * decay


@functools.lru_cache(maxsize=None)
def _filter_features(seq):
    _, n1, n2 = _fft_plan(seq)
    n1h = n1 // 2
    t01 = np.linspace(0.0, 1.0, seq, dtype=np.float32)[:, None]
    ang = (2.0 * math.pi * np.arange(seq, dtype=np.float32)[:, None] / seq).astype(np.float32)
    bands = np.linspace(1e-4, POS_BANDS - 1, POS_BANDS, dtype=np.float32)[None, :]
    arg = (bands * ang).astype(np.float32)
    z = np.concatenate([t01, np.cos(arg), -np.sin(arg)], axis=-1).astype(np.float32)
    pos = (np.arange(n1h)[None, :] * n2 + np.arange(n2)[:, None]).reshape(-1)
    zp = np.zeros((seq, FEAT_PAD), np.float32)
    zp[:, :z.shape[1]] = z[pos]
    return zp


def _filtergen(seq, w1, b1, f1, w2, b2, f2, w3, b3, f3, w4, hy_ch):
    fw = w1.shape[1]
    fo = w4.shape[1]
    z = jnp.asarray(_filter_features(seq))
    pe = z.shape[1]
    w1 = jnp.pad(w1, ((0, pe - w1.shape[0]), (0, 0)))
    deltas = np.abs(np.linspace(math.log(DECAY_FAST) / DECAY_TARGET, math.log(DECAY_SLOW) / DECAY_TARGET,
                                hy_ch, dtype=np.float32))
    dl = jnp.asarray(np.tile(deltas, fo // hy_ch).reshape(1, fo))
    tr = min(512, seq)
    tc = min(512, fo)
    row = lambda a: a.reshape(1, -1)
    full = lambda shp: pl.BlockSpec(shp, lambda i, j: (0, 0))
    return pl.pallas_call(
        _filtergen_kernel,
        grid=(seq // tr, fo // tc),
        in_specs=[
            pl.BlockSpec((tr, pe), lambda i, j: (i, 0)),
            full((pe, fw)), full((1, fw)), full((1, fw)),
            full((fw, fw)), full((1, fw)), full((1, fw)),
            full((fw, fw)), full((1, fw)), full((1, fw)),
            pl.BlockSpec((fw, tc), lambda i, j: (0, j)),
            pl.BlockSpec((1, tc), lambda i, j: (0, j)),
        ],
        out_specs=pl.BlockSpec((tr, tc), lambda i, j: (i, j)),
        out_shape=jax.ShapeDtypeStruct((seq, fo), F32),
        scratch_shapes=[pltpu.VMEM((tr, fw), F32)],
        compiler_params=_cparams(("arbitrary", "arbitrary")),
        name="filtergen",
    )(z, w1, row(b1), row(f1), w2, row(b2), row(f2), w3, row(b3), row(f3), w4, dl)


def _fftfwd_kernel(h_ref, f1_ref, g2_ref, o_ref, a_scr, *, n1, n2, k1c, scale):
    kc = pl.program_id(1)

    @pl.when(kc == 0)
    def _():
        _fft_stage1(lambda s: h_ref[s].astype(BF16), f1_ref, a_scr, n1, n2)

    def body(kk, carry):
        k1 = kc * k1c + kk
        x = jnp.dot(g2_ref[kk], _load_freq_rows(a_scr, k1, n1, n2), preferred_element_type=F32)
        o_ref[kk] = x * scale
        return carry
    lax.fori_loop(0, k1c, body, 0, unroll=True)


def _fftfwd_real(hs, seq):
    n, n1, n2 = _fft_plan(seq)
    n1h = n1 // 2
    cols = hs.shape[-1]
    f1, _, g2, _ = _dft_constants(seq)
    f1r = jnp.asarray(f1[:, :n1h], BF16)
    g2b = jnp.asarray(g2, BF16)
    k1c = min(8, n1)
    return pl.pallas_call(
        functools.partial(_fftfwd_kernel, n1=n1, n2=n2, k1c=k1c, scale=1.0 / n),
        grid=(cols // LANES, n1 // k1c),
        in_specs=[
            pl.BlockSpec((n2, n1h, LANES), lambda c, k: (0, 0, c)),
            pl.BlockSpec((2 * n1, n1h), lambda c, k: (0, 0)),
            pl.BlockSpec((k1c, 2 * n2, 2 * n2), lambda c, k: (k, 0, 0)),
        ],
        out_specs=pl.BlockSpec((k1c, 2 * n2, LANES), lambda c, k: (k, 0, c)),
        out_shape=jax.ShapeDtypeStruct((n1, 2 * n2, cols), F32),
        scratch_shapes=[pltpu.VMEM((n2 * _pitch(n1), LANES), jnp.uint32)],
        compiler_params=_cparams(("arbitrary", "arbitrary")),
        name="fftfwd",
    )(hs.reshape(n2, n1h, cols), f1r, g2b)


def _hyena_kernel(u_ref, g_ref, skip_ref, hf_ref, hb_ref, f1_ref, f1i_ref, g2_ref, g2i_ref, o_ref, a_scr,
                  *z_scr, n1, n2, k1c):
    kc = pl.program_id(2)
    n1h = n1 // 2
    zp = _pitch(n1h)

    @pl.when(kc == 0)
    def _():
        _fft_stage1(lambda s: jnp.concatenate([u_ref[0, s], u_ref[1, s]], axis=0), f1_ref, a_scr, n1, n2)

    def body(kk, carry):
        k1 = kc * k1c + kk
        x = jnp.dot(g2_ref[kk], _load_freq_rows(a_scr, k1, n1, n2), preferred_element_type=F32)
        xr, xi = x[:n2], x[n2:]
        hf = hf_ref[kk]
        hb = hb_ref[kk]
        hr = hf[:n2] + hb[:n2]
        hi = hf[n2:] - hb[n2:]
        y = jnp.concatenate([xr * hr - xi * hi, xr * hi + xi * hr], axis=0).astype(BF16)
        bq = jnp.dot(g2i_ref[kk], y, preferred_element_type=F32)
        a_scr[pl.ds(k1, n2, stride=_pitch(n1)), :] = _pack_complex(bq[:n2], bq[n2:])
        return carry
    lax.fori_loop(0, k1c, body, 0, unroll=True)

    @pl.when(kc == pl.num_programs(2) - 1)
    def _():
        skip = skip_ref[...]

        def body3(s, carry):
            re, im = _unpack_complex(a_scr[pl.ds(pl.multiple_of(s * _pitch(n1), SUBLANES), n1), :])
            y = jnp.dot(f1i_ref[...], jnp.concatenate([re, im], axis=0), preferred_element_type=F32)
            for bb in range(2):
                yb = y[bb * n1h:(bb + 1) * n1h]
                z = g_ref[bb, s].astype(F32) * (yb + skip * u_ref[bb, s].astype(F32))
                if z_scr:
                    z_scr[0][bb, pl.ds(pl.multiple_of(s * zp, SUBLANES), n1h), :] = z
                else:
                    o_ref[bb, s] = z.astype(o_ref.dtype)
            return carry
        lax.fori_loop(0, n2, body3, 0, unroll=FFT_UNROLL)

        if z_scr:
            def body4(r, carry):
                for bb in range(2):
                    rows = z_scr[0][bb, pl.ds(r, n2, stride=zp), :]
                    o_ref[bb, pl.ds(pl.multiple_of(r * n2, n2), n2), :] = rows.astype(o_ref.dtype)
                return carry
            lax.fori_loop(0, n1h, body4, 0, unroll=FFT_UNROLL)


def _hyena_conv(u_arr, u_col, g_arr, g_col, skip, tf, order, seq, hy_ch, token_order):
    b = u_arr.shape[0]
    n, n1, n2 = _fft_plan(seq)
    n1h = n1 // 2
    f1, f1i, g2, g2i = _dft_constants(seq)
    k1c = min(8, n1)
    ucb, gcb = u_col // LANES, g_col // LANES
    fcb, bcb = (2 * order) * hy_ch // LANES, (2 * order + 1) * hy_ch // LANES
    scratch = [pltpu.VMEM((n2 * _pitch(n1), LANES), jnp.uint32)]
    if token_order:
        out_spec = pl.BlockSpec((2, seq, LANES), lambda c, p, k: (p, 0, c))
        out_shape = jax.ShapeDtypeStruct((b, seq, hy_ch), BF16)
        scratch.append(pltpu.VMEM((2, n2 * _pitch(n1h), LANES), F32))
    else:
        out_spec = pl.BlockSpec((2, n2, n1h, LANES), lambda c, p, k: (p, 0, 0, c))
        out_shape = jax.ShapeDtypeStruct((b, n2, n1h, hy_ch), BF16)
    return pl.pallas_call(
        functools.partial(_hyena_kernel, n1=n1, n2=n2, k1c=k1c),
        grid=(hy_ch // LANES, b // 2, n1 // k1c),
        in_specs=[
            pl.BlockSpec((2, n2, n1h, LANES), lambda c, p, k: (p, 0, 0, ucb + c)),
            pl.BlockSpec((2, n2, n1h, LANES), lambda c, p, k: (p, 0, 0, gcb + c)),
            pl.BlockSpec((1, LANES), lambda c, p, k: (0, c)),
            pl.BlockSpec((k1c, 2 * n2, LANES), lambda c, p, k: (k, 0, fcb + c)),
            pl.BlockSpec((k1c, 2 * n2, LANES), lambda c, p, k: (k, 0, bcb + c)),
            pl.BlockSpec((2 * n1, n1), lambda c, p, k: (0, 0)),
            pl.BlockSpec((n1, 2 * n1), lambda c, p, k: (0, 0)),
            pl.BlockSpec((k1c, 2 * n2, 2 * n2), lambda c, p, k: (k, 0, 0)),
            pl.BlockSpec((k1c, 2 * n2, 2 * n2), lambda c, p, k: (k, 0, 0)),
        ],
        out_specs=out_spec,
        out_shape=out_shape,
        scratch_shapes=scratch,
        compiler_params=_cparams(("arbitrary", "arbitrary", "arbitrary")),
        name="hyenaconv",
    )(u_arr, g_arr, skip.reshape(1, hy_ch), tf, tf,
      jnp.asarray(f1, BF16), jnp.asarray(f1i, BF16), jnp.asarray(g2, BF16), jnp.asarray(g2i, BF16))


def _outproj_kernel(a_ref, z_ref, x_ref, wa_ref, wh_ref, gh_ref, gf_ref, wrh_ref, wrl_ref, br_ref, c0_ref,
                    tri_ref, x1_ref, xn_ref, idx_ref, gate_ref, rank_ref, cnt_ref, run_scr, *, n_exp):
    i = pl.program_id(0)

    @pl.when(i == 0)
    def _():
        run_scr[...] = c0_ref[...]

    z = z_ref[...].astype(F32)
    zn = (z * lax.rsqrt(jnp.mean(z * z, axis=-1, keepdims=True) + EPS) * gh_ref[...]).astype(BF16)
    mixed = (jnp.dot(a_ref[...], wa_ref[...], preferred_element_type=F32)
             + jnp.dot(zn, wh_ref[...], preferred_element_type=F32))
    x1 = x_ref[...] + mixed
    x1_ref[...] = x1
    xn = x1 * lax.rsqrt(jnp.mean(x1 * x1, axis=-1, keepdims=True) + EPS) * gf_ref[...]
    half = xn.shape[1] // 2
    packed = _pack_complex(xn[:, :half], xn[:, half:])
    for c in range(SUBLANES):
        xn_ref[pl.ds(c, xn.shape[0], stride=SUBLANES), :] = packed[:, c * LANES:(c + 1) * LANES]

    xh, xl = _split_bf16(xn)
    dg = functools.partial(lax.dot_general, dimension_numbers=NT_DIMS, preferred_element_type=F32)
    logit = dg(wrh_ref[...], xh) + dg(wrh_ref[...], xl) + dg(wrl_ref[...], xh) + br_ref[...]
    tm = logit.shape[1]
    eio = lax.broadcasted_iota(jnp.int32, (n_exp, tm), 0)
    vals, onehots = [], []
    cur = logit
    for k in range(TOP_K):
        m = jnp.max(cur, axis=0, keepdims=True)
        idx = jnp.min(jnp.where(cur == m, eio, n_exp), axis=0, keepdims=True)
        oh = eio == idx
        vals.append(m)
        onehots.append(oh)
        idx_ref[k:k + 1, :] = idx
        cur = jnp.where(oh, -jnp.inf, cur)
    ex = [jnp.exp(v - vals[0]) for v in vals]
    tot = ex[0] + ex[1] + ex[2] + ex[3]
    for k in range(TOP_K):
        gate_ref[k:k + 1, :] = ex[k] / tot
    ohsum = jnp.zeros((n_exp, tm), F32)
    for oh in onehots:
        ohsum = ohsum + oh.astype(F32)
    base = jnp.dot(ohsum.astype(BF16), tri_ref[...], preferred_element_type=F32) + run_scr[...]
    for k in range(TOP_K):
        rank_ref[k:k + 1, :] = jnp.sum(jnp.where(onehots[k], base, 0.0), axis=0, keepdims=True).astype(jnp.int32)
    run_scr[...] = run_scr[...] + jnp.sum(ohsum, axis=1, keepdims=True)
    cnt_ref[...] = run_scr[...]


def _outproj_router(attn_n, z2, x2d, wa, wh, g_hy, g_ffn, wr_hi, wr_lo, b_router, count0):
    t, d = x2d.shape
    aw, hc = attn_n.shape[1], z2.shape[1]
    n_exp = wr_hi.shape[0]
    tm = min(512, t)
    tri = jnp.asarray(np.triu(np.ones((tm, tm), np.float32), k=1), BF16)
    full = lambda shp: pl.BlockSpec(shp, lambda i: (0, 0))
    resident = lambda shp: pl.BlockSpec(shp, lambda i: (0, 0), pipeline_mode=pl.Buffered(1))
    rowblk = lambda w: pl.BlockSpec((tm, w), lambda i: (i, 0))
    colblk = pl.BlockSpec((TOP_K, tm), lambda i: (0, i))
    return pl.pallas_call(
        functools.partial(_outproj_kernel, n_exp=n_exp),
        grid=(t // tm,),
        in_specs=[rowblk(aw), rowblk(hc), rowblk(d), resident((aw, d)), resident((hc, d)), full((1, hc)),
                  full((1, d)), full((n_exp, d)), full((n_exp, d)), full((n_exp, 1)), full((n_exp, 1)),
                  resident((tm, tm))],
        out_specs=[rowblk(d), pl.BlockSpec((tm * SUBLANES, LANES), lambda i: (i, 0)),
                   colblk, colblk, colblk, full((n_exp, 1))],
        out_shape=[jax.ShapeDtypeStruct((t, d), F32), jax.ShapeDtypeStruct((t * SUBLANES, LANES), jnp.uint32),
                   jax.ShapeDtypeStruct((TOP_K, t), jnp.int32), jax.ShapeDtypeStruct((TOP_K, t), F32),
                   jax.ShapeDtypeStruct((TOP_K, t), jnp.int32), jax.ShapeDtypeStruct((n_exp, 1), F32)],
        scratch_shapes=[pltpu.VMEM((n_exp, 1), F32)],
        compiler_params=_cparams(("arbitrary",)),
        name="outproj_router",
    )(attn_n, z2, x2d, wa, wh, g_hy.reshape(1, hc), g_ffn.reshape(1, d), wr_hi, wr_lo,
      b_router.reshape(n_exp, 1), count0, tri)


def _dest_kernel(ps_ref, idx_ref, rank_ref, o_ref, *, n_exp):
    idx = idx_ref[...]
    acc = rank_ref[...]
    for e in range(n_exp):
        acc = acc + jnp.where(idx == e, ps_ref[e], 0)
    o_ref[...] = acc


def _dest_rows(pstart, idx, rank):
    k, t = idx.shape
    tb = min(2048, t)
    blk = pl.BlockSpec((k, tb), lambda i: (0, i))
    return pl.pallas_call(
        functools.partial(_dest_kernel, n_exp=pstart.shape[0]),
        grid=(t // tb,),
        in_specs=[pl.BlockSpec(memory_space=pltpu.SMEM), blk, blk],
        out_specs=blk,
        out_shape=jax.ShapeDtypeStruct((k, t), jnp.int32),
        compiler_params=_cparams(("arbitrary",)),
        name="dest",
    )(pstart, idx, rank)


def _dispatch_kernel(dest_ref, xn_ref, xs_in_ref, xs_ref, sem):
    del xs_in_ref
    tm = xn_ref.shape[0] // SUBLANES

    def row_copy(t, d):
        src = xn_ref.at[pl.ds(pl.multiple_of(t * SUBLANES, SUBLANES), SUBLANES)]
        return pltpu.make_async_copy(src, xs_ref.at[d], sem)

    def issue(t, carry):
        for k in range(TOP_K):
            row_copy(t, dest_ref[k, t]).start(priority=k % 2)
        return carry
    lax.fori_loop(0, tm, issue, 0)

    def drain(t, carry):
        for k in range(TOP_K):
            row_copy(0, 0).wait()
        return carry
    lax.fori_loop(0, tm, drain, 0)


def _dispatch(dest, xn, xs):
    t = dest.shape[1]
    tm = min(256, t)
    return pl.pallas_call(
        _dispatch_kernel,
        grid=(t // tm,),
        in_specs=[pl.BlockSpec((TOP_K, tm), lambda i: (0, i), memory_space=pltpu.SMEM),
                  pl.BlockSpec((tm * SUBLANES, LANES), lambda i: (i, 0)),
                  pl.BlockSpec(memory_space=pl.ANY)],
        out_specs=pl.BlockSpec(memory_space=pl.ANY),
        out_shape=jax.ShapeDtypeStruct(xs.shape, xs.dtype),
        scratch_shapes=[pltpu.SemaphoreType.DMA(())],
        input_output_aliases={2: 0},
        compiler_params=_cparams(("arbitrary",)),
        name="dispatch",
    )(dest, xn, xs)


def _expert_kernel(be_ref, nu_ref, x_ref, wg_ref, wu_ref, bg_ref, bu_ref, wd_ref, bd_ref, y_ref, x_scr, acc_scr):
    del be_ref
    i, j = pl.program_id(0), pl.program_id(1)
    bm, d = x_scr.shape
    half = d // 2

    @pl.when(i < nu_ref[0])
    def _():
        @pl.when(j == 0)
        def _():
            for c in range(SUBLANES):
                hi, lo = _unpack_complex(x_ref[pl.ds(c, bm, stride=SUBLANES), :])
                x_scr[:, c * LANES:(c + 1) * LANES] = hi
                x_scr[:, half + c * LANES:half + (c + 1) * LANES] = lo

        x = x_scr[...]
        g = jnp.dot(x, wg_ref[0], preferred_element_type=F32) + bg_ref[0]
        u = jnp.dot(x, wu_ref[0], preferred_element_type=F32) + bu_ref[0]
        glu = jnp.minimum(g, SWIGLU_LIMIT)
        lin = jnp.clip(u, -SWIGLU_LIMIT, SWIGLU_LIMIT)
        h = glu * jax.nn.sigmoid(SWIGLU_ALPHA * glu) * (lin + 1.0)
        part = jnp.dot(h.astype(BF16), wd_ref[0], preferred_element_type=F32)

        @pl.when(j == 0)
        def _():
            acc_scr[...] = part + bd_ref[0]

        @pl.when(j > 0)
        def _():
            acc_scr[...] = acc_scr[...] + part

        @pl.when(j == pl.num_programs(1) - 1)
        def _():
            for c in range(SUBLANES):
                y_ref[pl.ds(c, bm, stride=SUBLANES), :] = _pack_complex(
                    acc_scr[:, c * LANES:(c + 1) * LANES], acc_scr[:, half + c * LANES:half + (c + 1) * LANES])


def _experts(block_expert, n_used, xs, w_gu, b_gu, w_dn, b_dn):
    n_exp, d, f2 = w_gu.shape
    n_rows = xs.shape[0] // SUBLANES
    f = f2 // 2
    tf = next(c for c in (1024, 512, 256) if f % c == 0)
    assert f % tf == 0
    nf = f // tf
    bm = MOE_ROWS
    last = lambda i, nu: jnp.minimum(i, nu[0] - 1)
    fcol = lambda i, j, nu: jnp.where(i < nu[0], j, nf - 1)
    grid_spec = pltpu.PrefetchScalarGridSpec(
        num_scalar_prefetch=2,
        grid=(n_rows // bm, nf),
        in_specs=[
            pl.BlockSpec((bm * SUBLANES, LANES), lambda i, j, be, nu: (last(i, nu), 0)),
            pl.BlockSpec((1, d, tf), lambda i, j, be, nu: (be[i], 0, fcol(i, j, nu))),
            pl.BlockSpec((1, d, tf), lambda i, j, be, nu: (be[i], 0, nf + fcol(i, j, nu))),
            pl.BlockSpec((1, 1, tf), lambda i, j, be, nu: (be[i], 0, fcol(i, j, nu))),
            pl.BlockSpec((1, 1, tf), lambda i, j, be, nu: (be[i], 0, nf + fcol(i, j, nu))),
            pl.BlockSpec((1, tf, d), lambda i, j, be, nu: (be[i], fcol(i, j, nu), 0)),
            pl.BlockSpec((1, 1, d), lambda i, j, be, nu: (be[i], 0, 0)),
        ],
        out_specs=pl.BlockSpec((bm * SUBLANES, LANES), lambda i, j, be, nu: (last(i, nu), 0)),
        scratch_shapes=[pltpu.VMEM((bm, d), BF16), pltpu.VMEM((bm, d), F32)],
    )
    return pl.pallas_call(
        _expert_kernel,
        grid_spec=grid_spec,
        out_shape=jax.ShapeDtypeStruct((n_rows * SUBLANES, LANES), jnp.uint32),
        compiler_params=_cparams(("arbitrary", "arbitrary")),
        name="experts",
    )(block_expert, n_used, xs, w_gu, w_gu, b_gu.reshape(n_exp, 1, f2), b_gu.reshape(n_exp, 1, f2),
      w_dn, b_dn.reshape(n_exp, 1, d))


def _combine_kernel(dest_ref, x1_ref, gate_ref, ys_ref, o_ref, buf, sem):
    tm, d = x1_ref.shape
    half = d // 2

    def row_copy(t, k, d_row):
        dst = buf.at[k, pl.ds(pl.multiple_of(t * SUBLANES, SUBLANES), SUBLANES)]
        return pltpu.make_async_copy(ys_ref.at[d_row], dst, sem)

    def issue(t, carry):
        for k in range(TOP_K):
            row_copy(t, k, dest_ref[k, t]).start(priority=k % 2)
        return carry
    lax.fori_loop(0, tm, issue, 0)

    def drain(t, carry):
        for k in range(TOP_K):
            row_copy(0, 0, 0).wait()
        return carry
    lax.fori_loop(0, tm, drain, 0)

    gate = gate_ref[...]
    for c in range(SUBLANES):
        lo_cols = slice(c * LANES, (c + 1) * LANES)
        hi_cols = slice(half + c * LANES, half + (c + 1) * LANES)
        moe_a = x1_ref[:, lo_cols]
        moe_b = x1_ref[:, hi_cols]
        for k in range(TOP_K):
            w = buf[k, pl.ds(c, tm, stride=SUBLANES), :]
            g = gate[:, k:k + 1]
            moe_a = moe_a + g * lax.bitcast_convert_type(w & jnp.uint32(0xFFFF0000), F32)
            moe_b = moe_b + g * lax.bitcast_convert_type(w << 16, F32)
        o_ref[:, lo_cols] = moe_a
        o_ref[:, hi_cols] = moe_b


def _combine(dest, x1, gate_tk, ys):
    t, d = x1.shape
    tm = min(256, t)
    return pl.pallas_call(
        _combine_kernel,
        grid=(t // tm,),
        in_specs=[pl.BlockSpec((TOP_K, tm), lambda i: (0, i), memory_space=pltpu.SMEM),
                  pl.BlockSpec((tm, d), lambda i: (i, 0)),
                  pl.BlockSpec((tm, TOP_K), lambda i: (i, 0)),
                  pl.BlockSpec(memory_space=pl.ANY)],
        out_specs=pl.BlockSpec((tm, d), lambda i: (i, 0)),
        out_shape=jax.ShapeDtypeStruct((t, d), F32),
        scratch_shapes=[pltpu.VMEM((TOP_K, tm * SUBLANES, LANES), jnp.uint32), pltpu.SemaphoreType.DMA(())],
        compiler_params=_cparams(("arbitrary",)),
        name="combine",
    )(dest, x1, gate_tk, ys)


def _mixer(x3, g_mix, w_in_bf, g_q, g_k, sinks, w_short, b_short, flt, hyena_skip, g_attn_out, dims):
    n_heads, kv_w, hy_ch = dims
    b, seq, d = x3.shape
    attn_w = n_heads * HEAD_DIM
    proj = _inproj(x3.reshape(b * seq, d), g_mix, w_in_bf, g_q, g_k, attn_w, kv_w)
    proj3 = proj.reshape(b, seq, proj.shape[1])
    attn_n = _attention(proj3, sinks, g_attn_out, n_heads, kv_w)
    u = _shortconv(proj3, w_short, b_short, attn_w + 2 * kv_w)
    hs = _filtergen(seq, *flt, hy_ch)
    tf = _fftfwd_real(hs, seq)
    z1 = _hyena_conv(u, 0, u, hy_ch, hyena_skip[0], tf, 0, seq, hy_ch, token_order=False)
    z2 = _hyena_conv(z1, 0, u, 2 * hy_ch, hyena_skip[1], tf, 1, seq, hy_ch, token_order=True)
    return attn_n.reshape(b * seq, attn_w), z2.reshape(b * seq, hy_ch)


def kernel(x_prompt, x_sample, g_mix, w_in, g_q, g_k, attn_sinks, w_short, b_short, flt_w1, flt_b1, flt_freq1, flt_w2, flt_b2, flt_freq2, flt_w3, flt_b3, flt_freq3, flt_w4, hyena_skip, g_attn_out, g_hyena_out, w_out, g_ffn, w_router, b_router, w_gate_up, b_gate_up, w_down, b_down):
    assert g_mix.shape[0] == 1, "single-layer trunk"
    d = x_prompt.shape[-1]
    assert d == 2 * SUBLANES * LANES, "routed-row tile layout is built for d_model = 2048"
    n_heads = attn_sinks.shape[-1]
    hy_ch = hyena_skip.shape[-1]
    attn_w = n_heads * HEAD_DIM
    kv_w = (w_in.shape[-1] - attn_w - 3 * hy_ch) // 2
    n_exp = w_router.shape[-1]
    dims = (n_heads, kv_w, hy_ch)

    w_in_bf = w_in[0].astype(BF16)
    wa = w_out[0, :attn_w].astype(BF16)
    wh = w_out[0, attn_w:].astype(BF16)
    wr_t = w_router[0].T
    wr_hi = wr_t.astype(BF16)
    wr_lo = (wr_t - wr_hi.astype(F32)).astype(BF16)
    w_gu = w_gate_up[0].astype(BF16)
    w_dn = w_down[0].astype(BF16)
    flt = (flt_w1[0], flt_b1[0], flt_freq1[0], flt_w2[0], flt_b2[0], flt_freq2[0],
           flt_w3[0], flt_b3[0], flt_freq3[0], flt_w4[0])

    xs_in = (x_prompt, x_sample)
    routed = []
    counts = jnp.zeros((n_exp, 1), F32)
    for x3 in xs_in:
        attn_n, z2 = _mixer(x3, g_mix[0], w_in_bf, g_q[0], g_k[0], attn_sinks[0], w_short[0], b_short[0],
                            flt, hyena_skip[0], g_attn_out[0], dims)
        x1, xn, idx, gate, rank, counts = _outproj_router(
            attn_n, z2, x3.reshape(-1, d), wa, wh, g_hyena_out[0], g_ffn[0], wr_hi, wr_lo, b_router[0], counts)
        routed.append((x1, xn, idx, gate, rank))

    n_slots = sum(r[0].shape[0] for r in routed) * TOP_K
    n_rows = (n_slots // MOE_ROWS + n_exp) * MOE_ROWS
    cnt = counts[:, 0].astype(jnp.int32)
    padded = (cnt + MOE_ROWS - 1) // MOE_ROWS * MOE_ROWS
    pend = jnp.cumsum(padded)
    pstart = (pend - padded).astype(jnp.int32)
    n_blocks = n_rows // MOE_ROWS
    block_row0 = jnp.arange(n_blocks, dtype=jnp.int32) * MOE_ROWS
    block_expert = jnp.minimum(jnp.sum(pend[None, :] <= block_row0[:, None], axis=1), n_exp - 1).astype(jnp.int32)
    n_used = (pend[-1:] // MOE_ROWS).astype(jnp.int32)

    xs = jnp.zeros((n_rows, SUBLANES, LANES), jnp.uint32)
    dests = []
    for (x1, xn, idx, gate, rank) in routed:
        dest = _dest_rows(pstart, idx, rank)
        xs = _dispatch(dest, xn, xs)
        dests.append(dest)
    ys = _experts(block_expert, n_used, xs.reshape(n_rows * SUBLANES, LANES), w_gu, b_gate_up[0], w_dn, b_down[0])
    ys = ys.reshape(n_rows, SUBLANES, LANES)
    outs = []
    for (x1, xn, idx, gate, rank), dest, x3 in zip(routed, dests, xs_in):
        outs.append(_combine(dest, x1, gate.T, ys).reshape(x3.shape))
    return tuple(outs)
```

```python
import functools
import math

import numpy as np
import jax
import jax.numpy as jnp
from jax import lax
from jax.experimental import pallas as pl
from jax.experimental.pallas import tpu as pltpu

HEAD_DIM = 64
WINDOW = 128
TOP_K = 4
EPS = 1e-6
POS_BANDS = 16
FEAT_PAD = 64
DECAY_FAST = 0.3
DECAY_SLOW = 1.5
DECAY_TARGET = 1e-2
SWIGLU_ALPHA = 1.702
SWIGLU_LIMIT = 7.0

LANES = 128
SUBLANES = 8
VMEM_LIMIT = 56 * 1024 * 1024
MOE_ROWS = 512
FFT_UNROLL = 4
STAGE2_UNROLL = 8
F32 = jnp.float32
BF16 = jnp.bfloat16
NT_DIMS = (((1,), (1,)), ((), ()))


def _cparams(sem):
    return pltpu.CompilerParams(dimension_semantics=sem, vmem_limit_bytes=VMEM_LIMIT)


def _split_bf16(a):
    hi = a.astype(BF16)
    lo = (a - hi.astype(F32)).astype(BF16)
    return hi, lo


def _dot3(a, b):
    ah, al = _split_bf16(a)
    bh, bl = _split_bf16(b)
    d = functools.partial(jnp.dot, preferred_element_type=F32)
    return d(ah, bh) + d(ah, bl) + d(al, bh)


def _inproj_kernel(x_ref, g_ref, w_ref, gq_ref, gk_ref, s_ref, o_ref, h_scr, *, n_q_tiles):
    j = pl.program_id(1)

    @pl.when(j == 0)
    def _():
        x = x_ref[...]
        ms = jnp.mean(x * x, axis=-1, keepdims=True)
        h_scr[...] = (x * lax.rsqrt(ms + EPS) * g_ref[...]).astype(BF16)

    acc = jnp.dot(h_scr[...], w_ref[...], preferred_element_type=F32)
    tn = acc.shape[1]

    def head_norm(a, g):
        hi, lo = _split_bf16(a * a)
        s = s_ref[:a.shape[1], :a.shape[1]]
        ss = (jnp.dot(hi, s, preferred_element_type=F32)
              + jnp.dot(lo, s, preferred_element_type=F32))
        return a * lax.rsqrt(ss * (1.0 / HEAD_DIM) + EPS) * g

    @pl.when(j < n_q_tiles)
    def _():
        o_ref[...] = head_norm(acc, gq_ref[...]).astype(o_ref.dtype)

    @pl.when(j == n_q_tiles)
    def _():
        half = tn // 2
        o_ref[:, :half] = head_norm(acc[:, :half], gk_ref[...]).astype(o_ref.dtype)
        o_ref[:, half:] = acc[:, half:].astype(o_ref.dtype)

    @pl.when(j > n_q_tiles)
    def _():
        o_ref[...] = acc.astype(o_ref.dtype)


def _inproj(x2d, g_mix, w_in_bf, g_q, g_k, attn_w, kv_w):
    t, d = x2d.shape
    in_w = w_in_bf.shape[1]
    tn = 2 * kv_w
    tm = min(1024, t)
    assert attn_w % tn == 0 and in_w % tn == 0 and t % tm == 0
    n_q_tiles = attn_w // tn
    heads_per_tile = tn // HEAD_DIM
    blk = np.kron(np.eye(heads_per_tile, dtype=np.float32), np.ones((HEAD_DIM, HEAD_DIM), np.float32))
    gq_t = jnp.tile(g_q.reshape(1, HEAD_DIM), (1, heads_per_tile))
    gk_t = jnp.tile(g_k.reshape(1, HEAD_DIM), (1, heads_per_tile // 2))
    return pl.pallas_call(
        functools.partial(_inproj_kernel, n_q_tiles=n_q_tiles),
        grid=(t // tm, in_w // tn),
        in_specs=[
            pl.BlockSpec((tm, d), lambda i, j: (i, 0)),
            pl.BlockSpec((1, d), lambda i, j: (0, 0)),
            pl.BlockSpec((d, tn), lambda i, j: (0, j)),
            pl.BlockSpec((1, tn), lambda i, j: (0, 0)),
            pl.BlockSpec((1, tn // 2), lambda i, j: (0, 0)),
            pl.BlockSpec((tn, tn), lambda i, j: (0, 0)),
        ],
        out_specs=pl.BlockSpec((tm, tn), lambda i, j: (i, j)),
        out_shape=jax.ShapeDtypeStruct((t, in_w), BF16),
        scratch_shapes=[pltpu.VMEM((tm, d), BF16)],
        compiler_params=_cparams(("arbitrary", "arbitrary")),
        name="inproj",
    )(x2d, g_mix.reshape(1, d), w_in_bf, gq_t, gk_t, jnp.asarray(blk, BF16))


def _kv_start(i, tq, span, seq):
    return jnp.clip(i * tq - WINDOW, 0, seq - span)


def _attn_kernel(q_ref, k_ref, v_ref, bias_ref, sink_ref, g_ref, o_ref, o_scr, s_scr, p_scr,
                 *, n_heads, n_kv, tq, span, seq):
    i = pl.program_id(1)
    grp = n_heads // n_kv
    rows = 2 * SUBLANES
    start = pl.multiple_of(_kv_start(i, tq, span, seq), WINDOW)
    for kh in range(n_kv):
        kk = k_ref[0, pl.ds(start, span), kh * HEAD_DIM:(kh + 1) * HEAD_DIM] * (HEAD_DIM ** -0.5)
        for h in range(kh * grp, (kh + 1) * grp):
            q = q_ref[0, :, h * HEAD_DIM:(h + 1) * HEAD_DIM]
            s_scr[h * tq:(h + 1) * tq, :] = (lax.dot_general(q, kk, NT_DIMS, preferred_element_type=F32)
                                             + bias_ref[0, h * tq:(h + 1) * tq, :])
    for h in range(n_heads):
        sink = sink_ref[h]
        for r in range(h * tq, (h + 1) * tq, rows):
            s = s_scr[r:r + rows, :]
            m = jnp.maximum(jnp.max(s, axis=-1, keepdims=True), sink)
            p = jnp.exp(s - m)
            denom = jnp.sum(p, axis=-1, keepdims=True) + jnp.exp(sink - m)
            p_scr[r:r + rows, :] = (p / denom).astype(BF16)
    for kh in range(n_kv):
        vv = v_ref[0, pl.ds(start, span), kh * HEAD_DIM:(kh + 1) * HEAD_DIM]
        for h in range(kh * grp, (kh + 1) * grp):
            o_scr[:, h * HEAD_DIM:(h + 1) * HEAD_DIM] = jnp.dot(p_scr[h * tq:(h + 1) * tq, :], vv,
                                                                preferred_element_type=F32)
    a = o_scr[...]
    ms = jnp.mean(a * a, axis=-1, keepdims=True)
    o_ref[0] = (a * lax.rsqrt(ms + EPS) * g_ref[...]).astype(o_ref.dtype)


@functools.lru_cache(maxsize=None)
def _alibi_window_bias(n_heads, tq, span):
    slopes = np.exp2(-8.0 * np.arange(1, n_heads + 1, dtype=np.float32) / n_heads).astype(np.float32)
    r = np.arange(tq)[:, None]
    c = np.arange(span)[None, :]
    tables = []
    for v in range(3):
        dist = np.abs(r - c + v * WINDOW)
        bias = np.where(dist <= WINDOW, -slopes[:, None, None] * dist.astype(np.float32), -np.inf)
        tables.append(bias.reshape(n_heads * tq, span))
    return np.stack(tables).astype(np.float32)


def _attention(proj3, sinks, g_attn_out, n_heads, kv_w):
    b, seq, _ = proj3.shape
    attn_w = n_heads * HEAD_DIM
    n_kv = kv_w // HEAD_DIM
    tq = WINDOW
    span = min(3 * WINDOW, seq)
    kcol = attn_w // kv_w
    bias = jnp.asarray(_alibi_window_bias(n_heads, tq, span))
    return pl.pallas_call(
        functools.partial(_attn_kernel, n_heads=n_heads, n_kv=n_kv, tq=tq, span=span, seq=seq),
        grid=(b, seq // tq),
        in_specs=[
            pl.BlockSpec((1, tq, attn_w), lambda bi, i: (bi, i, 0)),
            pl.BlockSpec((1, seq, kv_w), lambda bi, i: (bi, 0, kcol)),
            pl.BlockSpec((1, seq, kv_w), lambda bi, i: (bi, 0, kcol + 1)),
            pl.BlockSpec((1, n_heads * tq, span),
                         lambda bi, i: ((i * tq - _kv_start(i, tq, span, seq)) // WINDOW, 0, 0)),
            pl.BlockSpec(memory_space=pltpu.SMEM),
            pl.BlockSpec((1, attn_w), lambda bi, i: (0, 0)),
        ],
        out_specs=pl.BlockSpec((1, tq, attn_w), lambda bi, i: (bi, i, 0)),
        out_shape=jax.ShapeDtypeStruct((b, seq, attn_w), BF16),
        scratch_shapes=[pltpu.VMEM((tq, attn_w), F32), pltpu.VMEM((n_heads * tq, span), F32),
                        pltpu.VMEM((n_heads * tq, span), BF16)],
        compiler_params=_cparams(("arbitrary", "arbitrary")),
        name="attention",
    )(proj3, proj3, proj3, bias, sinks.astype(F32), g_attn_out.reshape(1, attn_w))


def _fft_plan(seq):
    n = 2 * seq
    n2 = 1 << (int(math.log2(n)) // 2)
    n1 = n // n2
    return n, n1, n2


@functools.lru_cache(maxsize=None)
def _dft_constants(seq):
    n, n1, n2 = _fft_plan(seq)
    n1h = n1 // 2

    def blockc(re, im):
        return np.block([[re, -im], [im, re]])

    k1 = np.arange(n1)[:, None]
    a1 = 2.0 * np.pi * (k1 * np.arange(n1h)[None, :]) / n1
    f1 = blockc(np.cos(a1), -np.sin(a1))
    f1i = blockc(np.cos(a1.T), np.sin(a1.T))
    kk1 = np.arange(n1)[:, None, None]
    kk2 = np.arange(n2)[None, :, None]
    nn2 = np.arange(n2)[None, None, :]
    a2 = 2.0 * np.pi * (nn2 * (kk1 + n1 * kk2)) / n
    c2, s2 = np.cos(a2), np.sin(a2)
    g2 = np.concatenate([np.concatenate([c2, s2], axis=2),
                         np.concatenate([-s2, c2], axis=2)], axis=1)
    c2t, s2t = np.swapaxes(c2, 1, 2), np.swapaxes(s2, 1, 2)
    g2i = np.concatenate([np.concatenate([c2t, -s2t], axis=2),
                          np.concatenate([s2t, c2t], axis=2)], axis=1)
    return (f1.astype(np.float32), f1i.astype(np.float32),
            g2.astype(np.float32), g2i.astype(np.float32))


def _pack_complex(re, im):
    rb = lax.bitcast_convert_type(re.astype(BF16).astype(F32), jnp.uint32)
    ib = lax.bitcast_convert_type(im.astype(BF16).astype(F32), jnp.uint32)
    return (rb & jnp.uint32(0xFFFF0000)) | (ib >> 16)


def _unpack_complex(w):
    re = lax.bitcast_convert_type(w & jnp.uint32(0xFFFF0000), F32).astype(BF16)
    im = lax.bitcast_convert_type(w << 16, F32).astype(BF16)
    return re, im


def _fft_stage1(load_slab, f1_ref, a_scr, n1, n2):
    def body(s, carry):
        a = jnp.dot(f1_ref[...], load_slab(s), preferred_element_type=F32)
        a_scr[pl.ds(pl.multiple_of(s * _pitch(n1), SUBLANES), n1), :] = _pack_complex(a[:n1], a[n1:])
        return carry
    lax.fori_loop(0, n2, body, 0, unroll=FFT_UNROLL)


def _pitch(n1):
    return n1 + SUBLANES


def _load_freq_rows(a_scr, k1, n1, n2):
    re, im = _unpack_complex(a_scr[pl.ds(k1, n2, stride=_pitch(n1)), :])
    return jnp.concatenate([re, im], axis=0)


def _shortconv_kernel(p_ref, w_ref, b_ref, o_ref, u_scr, *, seq, n2, chunk):
    n1h = seq // n2
    up = _pitch(n2)
    w0, w1, w2 = w_ref[0:1, :], w_ref[1:2, :], w_ref[2:3, :]
    bias = b_ref[...]
    halo = 16
    for c in range(seq // chunk):
        r0 = c * chunk
        lo = max(r0 - halo, 0)
        hi = min(r0 + chunk + halo, seq)
        win = p_ref[0, lo:hi, :].astype(F32)
        off = r0 - lo
        cur = win[off:off + chunk]
        if r0 > 0:
            prev = win[off - 1:off - 1 + chunk]
        else:
            prev = jnp.concatenate([jnp.zeros((1, LANES), F32), win[0:chunk - 1]], axis=0)
        if r0 + chunk < seq:
            nxt = win[off + 1:off + 1 + chunk]
        else:
            nxt = jnp.concatenate([win[off + 1:off + chunk], jnp.zeros((1, LANES), F32)], axis=0)
        u = prev * w0 + cur * w1 + nxt * w2 + bias
        for gi in range(chunk // n2):
            r1 = (r0 // n2 + gi) * up
            u_scr[r1:r1 + n2, :] = u[gi * n2:(gi + 1) * n2]

    def body(s, carry):
        o_ref[0, s] = u_scr[pl.ds(s, n1h, stride=up), :].astype(o_ref.dtype)
        return carry
    lax.fori_loop(0, n2, body, 0, unroll=FFT_UNROLL)


def _shortconv(proj3, w_short, b_short, col0):
    b, seq, _ = proj3.shape
    cu = w_short.shape[1]
    _, n1, n2 = _fft_plan(seq)
    n1h = n1 // 2
    chunk = min(512, seq)
    cb0 = col0 // LANES
    return pl.pallas_call(
        functools.partial(_shortconv_kernel, seq=seq, n2=n2, chunk=chunk),
        grid=(b, cu // LANES),
        in_specs=[
            pl.BlockSpec((1, seq, LANES), lambda bi, c: (bi, 0, cb0 + c)),
            pl.BlockSpec((3, LANES), lambda bi, c: (0, c)),
            pl.BlockSpec((1, LANES), lambda bi, c: (0, c)),
        ],
        out_specs=pl.BlockSpec((1, n2, n1h, LANES), lambda bi, c: (bi, 0, 0, c)),
        out_shape=jax.ShapeDtypeStruct((b, n2, n1h, cu), BF16),
        scratch_shapes=[pltpu.VMEM((n1h * _pitch(n2), LANES), F32)],
        compiler_params=_cparams(("arbitrary", "arbitrary")),
        name="shortconv",
    )(proj3, w_short, b_short.reshape(1, cu))


def _filtergen_kernel(z_ref, w1_ref, b1_ref, f1_ref, w2_ref, b2_ref, f2_ref, w3_ref, b3_ref, f3_ref,
                      w4_ref, dl_ref, o_ref, h_scr):
    j = pl.program_id(1)

    @pl.when(j == 0)
    def _():
        z = z_ref[...]
        h = jnp.sin(f1_ref[...] * (_dot3(z, w1_ref[...]) + b1_ref[...]))
        h = jnp.sin(f2_ref[...] * (_dot3(h, w2_ref[...]) + b2_ref[...]))
        h_scr[...] = jnp.sin(f3_ref[...] * (_dot3(h, w3_ref[...]) + b3_ref[...]))

    t01 = z_ref[:, 0:1]
    decay = jnp.exp(-t01 * dl_ref[...])
    o_ref[...] = _dot3(h_scr[...], w4_ref[...]) * decay


@functools.lru_cache(maxsize=None)
def _filter_features(seq):
    _, n1, n2 = _fft_plan(seq)
    n1h = n1 // 2
    t01 = np.linspace(0.0, 1.0, seq, dtype=np.float32)[:, None]
    ang = (2.0 * math.pi * np.arange(seq, dtype=np.float32)[:, None] / seq).astype(np.float32)
    bands = np.linspace(1e-4, POS_BANDS - 1, POS_BANDS, dtype=np.float32)[None, :]
    arg = (bands * ang).astype(np.float32)
    z = np.concatenate([t01, np.cos(arg), -np.sin(arg)], axis=-1).astype(np.float32)
    pos = (np.arange(n1h)[None, :] * n2 + np.arange(n2)[:, None]).reshape(-1)
    zp = np.zeros((seq, FEAT_PAD), np.float32)
    zp[:, :z.shape[1]] = z[pos]
    return zp


def _filtergen(seq, w1, b1, f1, w2, b2, f2, w3, b3, f3, w4, hy_ch):
    fw = w1.shape[1]
    fo = w4.shape[1]
    z = jnp.asarray(_filter_features(seq))
    pe = z.shape[1]
    w1 = jnp.pad(w1, ((0, pe - w1.shape[0]), (0, 0)))
    deltas = np.abs(np.linspace(math.log(DECAY_FAST) / DECAY_TARGET, math.log(DECAY_SLOW) / DECAY_TARGET,
                                hy_ch, dtype=np.float32))
    dl = jnp.asarray(np.tile(deltas, fo // hy_ch).reshape(1, fo))
    tr = min(512, seq)
    tc = min(512, fo)
    row = lambda a: a.reshape(1, -1)
    full = lambda shp: pl.BlockSpec(shp, lambda i, j: (0, 0))
    return pl.pallas_call(
        _filtergen_kernel,
        grid=(seq // tr, fo // tc),
        in_specs=[
            pl.BlockSpec((tr, pe), lambda i, j: (i, 0)),
            full((pe, fw)), full((1, fw)), full((1, fw)),
            full((fw, fw)), full((1, fw)), full((1, fw)),
            full((fw, fw)), full((1, fw)), full((1, fw)),
            pl.BlockSpec((fw, tc), lambda i, j: (0, j)),
            pl.BlockSpec((1, tc), lambda i, j: (0, j)),
        ],
        out_specs=pl.BlockSpec((tr, tc), lambda i, j: (i, j)),
        out_shape=jax.ShapeDtypeStruct((seq, fo), F32),
        scratch_shapes=[pltpu.VMEM((tr, fw), F32)],
        compiler_params=_cparams(("arbitrary", "arbitrary")),
        name="filtergen",
    )(z, w1, row(b1), row(f1), w2, row(b2), row(f2), w3, row(b3), row(f3), w4, dl)


def _filterfft_kernel(hf_ref, hb_ref, f1_ref, g2_ref, o_ref, af_scr, ab_scr, *, n1, n2, k1c, scale):
    kc = pl.program_id(1)

    @pl.when(kc == 0)
    def _():
        _fft_stage1(lambda s: hf_ref[s].astype(BF16), f1_ref, af_scr, n1, n2)
        _fft_stage1(lambda s: hb_ref[s].astype(BF16), f1_ref, ab_scr, n1, n2)

    def body(kk, carry):
        k1 = kc * k1c + kk
        rows = jnp.concatenate([_load_freq_rows(af_scr, k1, n1, n2), _load_freq_rows(ab_scr, k1, n1, n2)], axis=1)
        x = jnp.dot(g2_ref[kk], rows, preferred_element_type=F32)
        xf, xb = x[:, :LANES], x[:, LANES:]
        o_ref[kk] = jnp.concatenate([xf[:n2] + xb[:n2], xf[n2:] - xb[n2:]], axis=0) * scale
        return carry
    lax.fori_loop(0, k1c, body, 0, unroll=min(k1c, STAGE2_UNROLL))


def _filter_spectrum(hs, seq, hy_ch):
    n, n1, n2 = _fft_plan(seq)
    n1h = n1 // 2
    cols = hs.shape[-1]
    cpt = hy_ch // LANES
    f1, _, g2, _ = _dft_constants(seq)
    f1r = jnp.asarray(f1[:, :n1h], BF16)
    g2b = jnp.asarray(g2, BF16)
    k1c = _k1_chunk(n1)
    hs3 = hs.reshape(n2, n1h, cols)
    scratch = pltpu.VMEM((n2 * _pitch(n1), LANES), jnp.uint32)
    return pl.pallas_call(
        functools.partial(_filterfft_kernel, n1=n1, n2=n2, k1c=k1c, scale=1.0 / n),
        grid=(2 * cpt, n1 // k1c),
        in_specs=[
            pl.BlockSpec((n2, n1h, LANES), lambda c, k: (0, 0, (c // cpt) * 2 * cpt + c % cpt)),
            pl.BlockSpec((n2, n1h, LANES), lambda c, k: (0, 0, (c // cpt) * 2 * cpt + cpt + c % cpt)),
            pl.BlockSpec((2 * n1, n1h), lambda c, k: (0, 0)),
            pl.BlockSpec((k1c, 2 * n2, 2 * n2), lambda c, k: (k, 0, 0)),
        ],
        out_specs=pl.BlockSpec((k1c, 2 * n2, LANES), lambda c, k: (k, 0, c)),
        out_shape=jax.ShapeDtypeStruct((n1, 2 * n2, 2 * hy_ch), F32),
        scratch_shapes=[scratch, scratch],
        compiler_params=_cparams(("arbitrary", "arbitrary")),
        name="filterfft",
    )(hs3, hs3, f1r, g2b)


def _k1_chunk(n1):
    return n1 if n1 <= 64 else 8


def _hyena_kernel(u_ref, g_ref, skip_ref, h_ref, f1_ref, f1i_ref, g2_ref, g2i_ref, o_ref, a_scr,
                  *z_scr, n1, n2, k1c):
    kc = pl.program_id(2)
    n1h = n1 // 2
    zp = _pitch(n1h)

    @pl.when(kc == 0)
    def _():
        _fft_stage1(lambda s: jnp.concatenate([u_ref[0, s], u_ref[1, s]], axis=0), f1_ref, a_scr, n1, n2)

    def body(kk, carry):
        k1 = kc * k1c + kk
        x = jnp.dot(g2_ref[kk], _load_freq_rows(a_scr, k1, n1, n2), preferred_element_type=F32)
        xr, xi = x[:n2], x[n2:]
        h = h_ref[kk]
        hr, hi = h[:n2], h[n2:]
        y = jnp.concatenate([xr * hr - xi * hi, xr * hi + xi * hr], axis=0).astype(BF16)
        bq = jnp.dot(g2i_ref[kk], y, preferred_element_type=F32)
        a_scr[pl.ds(k1, n2, stride=_pitch(n1)), :] = _pack_complex(bq[:n2], bq[n2:])
        return carry
    lax.fori_loop(0, k1c, body, 0, unroll=min(k1c, STAGE2_UNROLL))

    @pl.when(kc == pl.num_programs(2) - 1)
    def _():
        skip = skip_ref[...]

        def body3(s, carry):
            re, im = _unpack_complex(a_scr[pl.ds(pl.multiple_of(s * _pitch(n1), SUBLANES), n1), :])
            y = jnp.dot(f1i_ref[...], jnp.concatenate([re, im], axis=0), preferred_element_type=F32)
            for bb in range(2):
                yb = y[bb * n1h:(bb + 1) * n1h]
                z = g_ref[bb, s].astype(F32) * (yb + skip * u_ref[bb, s].astype(F32))
                if z_scr:
                    z_scr[0][bb, pl.ds(pl.multiple_of(s * zp, SUBLANES), n1h), :] = z
                else:
                    o_ref[bb, s] = z.astype(o_ref.dtype)
            return carry
        lax.fori_loop(0, n2, body3, 0, unroll=FFT_UNROLL)

        if z_scr:
            def body4(r, carry):
                for bb in range(2):
                    rows = z_scr[0][bb, pl.ds(r, n2, stride=zp), :]
                    o_ref[bb, pl.ds(pl.multiple_of(r * n2, n2), n2), :] = rows.astype(o_ref.dtype)
                return carry
            lax.fori_loop(0, n1h, body4, 0, unroll=FFT_UNROLL)


def _hyena_conv(u_arr, u_col, g_arr, g_col, skip, tf, order, seq, hy_ch, token_order):
    b = u_arr.shape[0]
    n, n1, n2 = _fft_plan(seq)
    n1h = n1 // 2
    f1, f1i, g2, g2i = _dft_constants(seq)
    k1c = _k1_chunk(n1)
    ucb, gcb = u_col // LANES, g_col // LANES
    hcb = order * hy_ch // LANES
    scratch = [pltpu.VMEM((n2 * _pitch(n1), LANES), jnp.uint32)]
    if token_order:
        out_spec = pl.BlockSpec((2, seq, LANES), lambda c, p, k: (p, 0, c))
        out_shape = jax.ShapeDtypeStruct((b, seq, hy_ch), BF16)
        scratch.append(pltpu.VMEM((2, n2 * _pitch(n1h), LANES), F32))
    else:
        out_spec = pl.BlockSpec((2, n2, n1h, LANES), lambda c, p, k: (p, 0, 0, c))
        out_shape = jax.ShapeDtypeStruct((b, n2, n1h, hy_ch), BF16)
    return pl.pallas_call(
        functools.partial(_hyena_kernel, n1=n1, n2=n2, k1c=k1c),
        grid=(hy_ch // LANES, b // 2, n1 // k1c),
        in_specs=[
            pl.BlockSpec((2, n2, n1h, LANES), lambda c, p, k: (p, 0, 0, ucb + c)),
            pl.BlockSpec((2, n2, n1h, LANES), lambda c, p, k: (p, 0, 0, gcb + c)),
            pl.BlockSpec((1, LANES), lambda c, p, k: (0, c)),
            pl.BlockSpec((k1c, 2 * n2, LANES), lambda c, p, k: (k, 0, hcb + c)),
            pl.BlockSpec((2 * n1, n1), lambda c, p, k: (0, 0)),
            pl.BlockSpec((n1, 2 * n1), lambda c, p, k: (0, 0)),
            pl.BlockSpec((k1c, 2 * n2, 2 * n2), lambda c, p, k: (k, 0, 0)),
            pl.BlockSpec((k1c, 2 * n2, 2 * n2), lambda c, p, k: (k, 0, 0)),
        ],
        out_specs=out_spec,
        out_shape=out_shape,
        scratch_shapes=scratch,
        compiler_params=_cparams(("arbitrary", "arbitrary", "arbitrary")),
        name="hyenaconv",
    )(u_arr, g_arr, skip.reshape(1, hy_ch), tf,
      jnp.asarray(f1, BF16), jnp.asarray(f1i, BF16), jnp.asarray(g2, BF16), jnp.asarray(g2i, BF16))


def _outproj_kernel(a_ref, z_ref, x_ref, wa_ref, wh_ref, gh_ref, gf_ref, wrh_ref, wrl_ref, br_ref, c0_ref,
                    tri_ref, x1_ref, xn_ref, idx_ref, gate_ref, rank_ref, cnt_ref, run_scr, *, n_exp):
    i = pl.program_id(0)

    @pl.when(i == 0)
    def _():
        run_scr[...] = c0_ref[...]

    z = z_ref[...].astype(F32)
    zn = (z * lax.rsqrt(jnp.mean(z * z, axis=-1, keepdims=True) + EPS) * gh_ref[...]).astype(BF16)
    mixed = (jnp.dot(a_ref[...], wa_ref[...], preferred_element_type=F32)
             + jnp.dot(zn, wh_ref[...], preferred_element_type=F32))
    x1 = x_ref[...] + mixed
    x1_ref[...] = x1
    xn = x1 * lax.rsqrt(jnp.mean(x1 * x1, axis=-1, keepdims=True) + EPS) * gf_ref[...]
    half = xn.shape[1] // 2
    packed = _pack_complex(xn[:, :half], xn[:, half:])
    for c in range(SUBLANES):
        xn_ref[pl.ds(c, xn.shape[0], stride=SUBLANES), :] = packed[:, c * LANES:(c + 1) * LANES]

    xh, xl = _split_bf16(xn)
    dg = functools.partial(lax.dot_general, dimension_numbers=NT_DIMS, preferred_element_type=F32)
    logit = dg(wrh_ref[...], xh) + dg(wrh_ref[...], xl) + dg(wrl_ref[...], xh) + br_ref[...]
    tm = logit.shape[1]
    eio = lax.broadcasted_iota(jnp.int32, (n_exp, tm), 0)
    vals, onehots = [], []
    cur = logit
    for k in range(TOP_K):
        m = jnp.max(cur, axis=0, keepdims=True)
        idx = jnp.min(jnp.where(cur == m, eio, n_exp), axis=0, keepdims=True)
        oh = eio == idx
        vals.append(m)
        onehots.append(oh)
        idx_ref[k:k + 1, :] = idx
        cur = jnp.where(oh, -jnp.inf, cur)
    ex = [jnp.exp(v - vals[0]) for v in vals]
    tot = ex[0] + ex[1] + ex[2] + ex[3]
    for k in range(TOP_K):
        gate_ref[k:k + 1, :] = ex[k] / tot
    ohsum = jnp.zeros((n_exp, tm), F32)
    for oh in onehots:
        ohsum = ohsum + oh.astype(F32)
    base = jnp.dot(ohsum.astype(BF16), tri_ref[...], preferred_element_type=F32) + run_scr[...]
    for k in range(TOP_K):
        rank_ref[k:k + 1, :] = jnp.sum(jnp.where(onehots[k], base, 0.0), axis=0, keepdims=True).astype(jnp.int32)
    run_scr[...] = run_scr[...] + jnp.sum(ohsum, axis=1, keepdims=True)
    cnt_ref[...] = run_scr[...]


def _outproj_router(attn_n, z2, x2d, wa, wh, g_hy, g_ffn, wr_hi, wr_lo, b_router, count0):
    t, d = x2d.shape
    aw, hc = attn_n.shape[1], z2.shape[1]
    n_exp = wr_hi.shape[0]
    tm = min(512, t)
    tri = jnp.asarray(np.triu(np.ones((tm, tm), np.float32), k=1), BF16)
    full = lambda shp: pl.BlockSpec(shp, lambda i: (0, 0))
    resident = lambda shp: pl.BlockSpec(shp, lambda i: (0, 0), pipeline_mode=pl.Buffered(1))
    rowblk = lambda w: pl.BlockSpec((tm, w), lambda i: (i, 0))
    colblk = pl.BlockSpec((TOP_K, tm), lambda i: (0, i))
    return pl.pallas_call(
        functools.partial(_outproj_kernel, n_exp=n_exp),
        grid=(t // tm,),
        in_specs=[rowblk(aw), rowblk(hc), rowblk(d), resident((aw, d)), resident((hc, d)), full((1, hc)),
                  full((1, d)), full((n_exp, d)), full((n_exp, d)), full((n_exp, 1)), full((n_exp, 1)),
                  resident((tm, tm))],
        out_specs=[rowblk(d), pl.BlockSpec((tm * SUBLANES, LANES), lambda i: (i, 0)),
                   colblk, colblk, colblk, full((n_exp, 1))],
        out_shape=[jax.ShapeDtypeStruct((t, d), F32), jax.ShapeDtypeStruct((t * SUBLANES, LANES), jnp.uint32),
                   jax.ShapeDtypeStruct((TOP_K, t), jnp.int32), jax.ShapeDtypeStruct((TOP_K, t), F32),
                   jax.ShapeDtypeStruct((TOP_K, t), jnp.int32), jax.ShapeDtypeStruct((n_exp, 1), F32)],
        scratch_shapes=[pltpu.VMEM((n_exp, 1), F32)],
        compiler_params=_cparams(("arbitrary",)),
        name="outproj_router",
    )(attn_n, z2, x2d, wa, wh, g_hy.reshape(1, hc), g_ffn.reshape(1, d), wr_hi, wr_lo,
      b_router.reshape(n_exp, 1), count0, tri)


def _dest_kernel(ps_ref, idx_ref, rank_ref, o_ref, *, n_exp):
    idx = idx_ref[...]
    acc = rank_ref[...]
    for e in range(n_exp):
        acc = acc + jnp.where(idx == e, ps_ref[e], 0)
    o_ref[...] = acc


def _dest_rows(pstart, idx, rank):
    k, t = idx.shape
    tb = min(2048, t)
    blk = pl.BlockSpec((k, tb), lambda i: (0, i))
    return pl.pallas_call(
        functools.partial(_dest_kernel, n_exp=pstart.shape[0]),
        grid=(t // tb,),
        in_specs=[pl.BlockSpec(memory_space=pltpu.SMEM), blk, blk],
        out_specs=blk,
        out_shape=jax.ShapeDtypeStruct((k, t), jnp.int32),
        compiler_params=_cparams(("arbitrary",)),
        name="dest",
    )(pstart, idx, rank)


def _dispatch_kernel(dest_ref, xn_ref, xs_in_ref, xs_ref, sem):
    del xs_in_ref
    tm = xn_ref.shape[0] // SUBLANES

    def row_copy(t, d):
        src = xn_ref.at[pl.ds(pl.multiple_of(t * SUBLANES, SUBLANES), SUBLANES)]
        return pltpu.make_async_copy(src, xs_ref.at[d], sem)

    def issue(t, carry):
        for k in range(TOP_K):
            row_copy(t, dest_ref[k, t]).start(priority=k % 2)
        return carry
    lax.fori_loop(0, tm, issue, 0)

    def drain(t, carry):
        for k in range(TOP_K):
            row_copy(0, 0).wait()
        return carry
    lax.fori_loop(0, tm, drain, 0)


def _dispatch(dest, xn, xs):
    t = dest.shape[1]
    tm = min(256, t)
    return pl.pallas_call(
        _dispatch_kernel,
        grid=(t // tm,),
        in_specs=[pl.BlockSpec((TOP_K, tm), lambda i: (0, i), memory_space=pltpu.SMEM),
                  pl.BlockSpec((tm * SUBLANES, LANES), lambda i: (i, 0)),
                  pl.BlockSpec(memory_space=pl.ANY)],
        out_specs=pl.BlockSpec(memory_space=pl.ANY),
        out_shape=jax.ShapeDtypeStruct(xs.shape, xs.dtype),
        scratch_shapes=[pltpu.SemaphoreType.DMA(())],
        input_output_aliases={2: 0},
        compiler_params=_cparams(("arbitrary",)),
        name="dispatch",
    )(dest, xn, xs)


def _expert_kernel(be_ref, nu_ref, x_ref, wg_ref, wu_ref, bg_ref, bu_ref, wd_ref, bd_ref, y_ref, x_scr, acc_scr):
    del be_ref
    i, j = pl.program_id(0), pl.program_id(1)
    bm, d = x_scr.shape
    half = d // 2

    @pl.when(i < nu_ref[0])
    def _():
        @pl.when(j == 0)
        def _():
            for c in range(SUBLANES):
                hi, lo = _unpack_complex(x_ref[pl.ds(c, bm, stride=SUBLANES), :])
                x_scr[:, c * LANES:(c + 1) * LANES] = hi
                x_scr[:, half + c * LANES:half + (c + 1) * LANES] = lo

        x = x_scr[...]
        g = jnp.dot(x, wg_ref[0], preferred_element_type=F32) + bg_ref[0]
        u = jnp.dot(x, wu_ref[0], preferred_element_type=F32) + bu_ref[0]
        glu = jnp.minimum(g, SWIGLU_LIMIT)
        lin = jnp.clip(u, -SWIGLU_LIMIT, SWIGLU_LIMIT)
        h = glu * jax.nn.sigmoid(SWIGLU_ALPHA * glu) * (lin + 1.0)
        part = jnp.dot(h.astype(BF16), wd_ref[0], preferred_element_type=F32)

        @pl.when(j == 0)
        def _():
            acc_scr[...] = part + bd_ref[0]

        @pl.when(j > 0)
        def _():
            acc_scr[...] = acc_scr[...] + part

        @pl.when(j == pl.num_programs(1) - 1)
        def _():
            for c in range(SUBLANES):
                y_ref[pl.ds(c, bm, stride=SUBLANES), :] = _pack_complex(
                    acc_scr[:, c * LANES:(c + 1) * LANES], acc_scr[:, half + c * LANES:half + (c + 1) * LANES])


def _experts(block_expert, n_used, xs, w_gu, b_gu, w_dn, b_dn):
    n_exp, d, f2 = w_gu.shape
    n_rows = xs.shape[0] // SUBLANES
    f = f2 // 2
    tf = next(c for c in (1024, 512, 256) if f % c == 0)
    assert f % tf == 0
    nf = f // tf
    bm = MOE_ROWS
    last = lambda i, nu: jnp.minimum(i, nu[0] - 1)
    fcol = lambda i, j, nu: jnp.where(i < nu[0], j, nf - 1)
    grid_spec = pltpu.PrefetchScalarGridSpec(
        num_scalar_prefetch=2,
        grid=(n_rows // bm, nf),
        in_specs=[
            pl.BlockSpec((bm * SUBLANES, LANES), lambda i, j, be, nu: (last(i, nu), 0)),
            pl.BlockSpec((1, d, tf), lambda i, j, be, nu: (be[i], 0, fcol(i, j, nu))),
            pl.BlockSpec((1, d, tf), lambda i, j, be, nu: (be[i], 0, nf + fcol(i, j, nu))),
            pl.BlockSpec((1, 1, tf), lambda i, j, be, nu: (be[i], 0, fcol(i, j, nu))),
            pl.BlockSpec((1, 1, tf), lambda i, j, be, nu: (be[i], 0, nf + fcol(i, j, nu))),
            pl.BlockSpec((1, tf, d), lambda i, j, be, nu: (be[i], fcol(i, j, nu), 0)),
            pl.BlockSpec((1, 1, d), lambda i, j, be, nu: (be[i], 0, 0)),
        ],
        out_specs=pl.BlockSpec((bm * SUBLANES, LANES), lambda i, j, be, nu: (last(i, nu), 0)),
        scratch_shapes=[pltpu.VMEM((bm, d), BF16), pltpu.VMEM((bm, d), F32)],
    )
    return pl.pallas_call(
        _expert_kernel,
        grid_spec=grid_spec,
        out_shape=jax.ShapeDtypeStruct((n_rows * SUBLANES, LANES), jnp.uint32),
        compiler_params=_cparams(("arbitrary", "arbitrary")),
        name="experts",
    )(block_expert, n_used, xs, w_gu, w_gu, b_gu.reshape(n_exp, 1, f2), b_gu.reshape(n_exp, 1, f2),
      w_dn, b_dn.reshape(n_exp, 1, d))


def _combine_kernel(dest_ref, x1_ref, gate_ref, ys_ref, o_ref, buf, sem):
    tm, d = x1_ref.shape
    half = d // 2

    def row_copy(t, k, d_row):
        dst = buf.at[k, pl.ds(pl.multiple_of(t * SUBLANES, SUBLANES), SUBLANES)]
        return pltpu.make_async_copy(ys_ref.at[d_row], dst, sem)

    def issue(t, carry):
        for k in range(TOP_K):
            row_copy(t, k, dest_ref[k, t]).start(priority=k % 2)
        return carry
    lax.fori_loop(0, tm, issue, 0)

    def drain(t, carry):
        for k in range(TOP_K):
            row_copy(0, 0, 0).wait()
        return carry
    lax.fori_loop(0, tm, drain, 0)

    gate = gate_ref[...]
    for c in range(SUBLANES):
        lo_cols = slice(c * LANES, (c + 1) * LANES)
        hi_cols = slice(half + c * LANES, half + (c + 1) * LANES)
        moe_a = x1_ref[:, lo_cols]
        moe_b = x1_ref[:, hi_cols]
        for k in range(TOP_K):
            w = buf[k, pl.ds(c, tm, stride=SUBLANES), :]
            g = gate[:, k:k + 1]
            moe_a = moe_a + g * lax.bitcast_convert_type(w & jnp.uint32(0xFFFF0000), F32)
            moe_b = moe_b + g * lax.bitcast_convert_type(w << 16, F32)
        o_ref[:, lo_cols] = moe_a
        o_ref[:, hi_cols] = moe_b


def _combine(dest, x1, gate_tk, ys):
    t, d = x1.shape
    tm = min(256, t)
    return pl.pallas_call(
        _combine_kernel,
        grid=(t // tm,),
        in_specs=[pl.BlockSpec((TOP_K, tm), lambda i: (0, i), memory_space=pltpu.SMEM),
                  pl.BlockSpec((tm, d), lambda i: (i, 0)),
                  pl.BlockSpec((tm, TOP_K), lambda i: (i, 0)),
                  pl.BlockSpec(memory_space=pl.ANY)],
        out_specs=pl.BlockSpec((tm, d), lambda i: (i, 0)),
        out_shape=jax.ShapeDtypeStruct((t, d), F32),
        scratch_shapes=[pltpu.VMEM((TOP_K, tm * SUBLANES, LANES), jnp.uint32), pltpu.SemaphoreType.DMA(())],
        compiler_params=_cparams(("arbitrary",)),
        name="combine",
    )(dest, x1, gate_tk, ys)


def _mixer(x3, g_mix, w_in_bf, g_q, g_k, sinks, w_short, b_short, flt, hyena_skip, g_attn_out, dims):
    n_heads, kv_w, hy_ch = dims
    b, seq, d = x3.shape
    attn_w = n_heads * HEAD_DIM
    proj = _inproj(x3.reshape(b * seq, d), g_mix, w_in_bf, g_q, g_k, attn_w, kv_w)
    proj3 = proj.reshape(b, seq, proj.shape[1])
    attn_n = _attention(proj3, sinks, g_attn_out, n_heads, kv_w)
    u = _shortconv(proj3, w_short, b_short, attn_w + 2 * kv_w)
    hs = _filtergen(seq, *flt, hy_ch)
    tf = _filter_spectrum(hs, seq, hy_ch)
    z1 = _hyena_conv(u, 0, u, hy_ch, hyena_skip[0], tf, 0, seq, hy_ch, token_order=False)
    z2 = _hyena_conv(z1, 0, u, 2 * hy_ch, hyena_skip[1], tf, 1, seq, hy_ch, token_order=True)
    return attn_n.reshape(b * seq, attn_w), z2.reshape(b * seq, hy_ch)


def kernel(x_prompt, x_sample, g_mix, w_in, g_q, g_k, attn_sinks, w_short, b_short, flt_w1, flt_b1, flt_freq1, flt_w2, flt_b2, flt_freq2, flt_w3, flt_b3, flt_freq3, flt_w4, hyena_skip, g_attn_out, g_hyena_out, w_out, g_ffn, w_router, b_router, w_gate_up, b_gate_up, w_down, b_down):
    assert g_mix.shape[0] == 1, "single-layer trunk"
    d = x_prompt.shape[-1]
    assert d == 2 * SUBLANES * LANES, "routed-row tile layout is built for d_model = 2048"
    n_heads = attn_sinks.shape[-1]
    hy_ch = hyena_skip.shape[-1]
    attn_w = n_heads * HEAD_DIM
    kv_w = (w_in.shape[-1] - attn_w - 3 * hy_ch) // 2
    n_exp = w_router.shape[-1]
    dims = (n_heads, kv_w, hy_ch)

    w_in_bf = w_in[0].astype(BF16)
    wa = w_out[0, :attn_w].astype(BF16)
    wh = w_out[0, attn_w:].astype(BF16)
    wr_t = w_router[0].T
    wr_hi = wr_t.astype(BF16)
    wr_lo = (wr_t - wr_hi.astype(F32)).astype(BF16)
    w_gu = w_gate_up[0].astype(BF16)
    w_dn = w_down[0].astype(BF16)
    flt = (flt_w1[0], flt_b1[0], flt_freq1[0], flt_w2[0], flt_b2[0], flt_freq2[0],
           flt_w3[0], flt_b3[0], flt_freq3[0], flt_w4[0])

    xs_in = (x_prompt, x_sample)
    routed = []
    counts = jnp.zeros((n_exp, 1), F32)
    for x3 in xs_in:
        attn_n, z2 = _mixer(x3, g_mix[0], w_in_bf, g_q[0], g_k[0], attn_sinks[0], w_short[0], b_short[0],
                            flt, hyena_skip[0], g_attn_out[0], dims)
        x1, xn, idx, gate, rank, counts = _outproj_router(
            attn_n, z2, x3.reshape(-1, d), wa, wh, g_hyena_out[0], g_ffn[0], wr_hi, wr_lo, b_router[0], counts)
        routed.append((x1, xn, idx, gate, rank))

    n_slots = sum(r[0].shape[0] for r in routed) * TOP_K
    n_rows = (n_slots // MOE_ROWS + n_exp) * MOE_ROWS
    cnt = counts[:, 0].astype(jnp.int32)
    padded = (cnt + MOE_ROWS - 1) // MOE_ROWS * MOE_ROWS
    pend = jnp.cumsum(padded)
    pstart = (pend - padded).astype(jnp.int32)
    n_blocks = n_rows // MOE_ROWS
    block_row0 = jnp.arange(n_blocks, dtype=jnp.int32) * MOE_ROWS
    block_expert = jnp.minimum(jnp.sum(pend[None, :] <= block_row0[:, None], axis=1), n_exp - 1).astype(jnp.int32)
    n_used = (pend[-1:] // MOE_ROWS).astype(jnp.int32)

    xs = jnp.zeros((n_rows, SUBLANES, LANES), jnp.uint32)
    dests = []
    for (x1, xn, idx, gate, rank) in routed:
        dest = _dest_rows(pstart, idx, rank)
        xs = _dispatch(dest, xn, xs)
        dests.append(dest)
    ys = _experts(block_expert, n_used, xs.reshape(n_rows * SUBLANES, LANES), w_gu, b_gate_up[0], w_dn, b_down[0])
    ys = ys.reshape(n_rows, SUBLANES, LANES)
    outs = []
    for (x1, xn, idx, gate, rank), dest, x3 in zip(routed, dests, xs_in):
        outs.append(_combine(dest, x1, gate.T, ys).reshape(x3.shape))
    return tuple(outs)
```

```python
import functools
import math

import numpy as np
import jax
import jax.numpy as jnp
from jax import lax
from jax.experimental import pallas as pl
from jax.experimental.pallas import tpu as pltpu

HEAD_DIM = 64
WINDOW = 128
TOP_K = 4
EPS = 1e-6
POS_BANDS = 16
FEAT_PAD = 64
DECAY_FAST = 0.3
DECAY_SLOW = 1.5
DECAY_TARGET = 1e-2
SWIGLU_ALPHA = 1.702
SWIGLU_LIMIT = 7.0

LANES = 128
SUBLANES = 8
VMEM_LIMIT = 56 * 1024 * 1024
MOE_ROWS = 512
FFT_UNROLL = 8
STAGE2_UNROLL = 8
F32 = jnp.float32
BF16 = jnp.bfloat16
NT_DIMS = (((1,), (1,)), ((), ()))


def _cparams(sem):
    return pltpu.CompilerParams(dimension_semantics=sem, vmem_limit_bytes=VMEM_LIMIT)


def _split_bf16(a):
    hi = a.astype(BF16)
    lo = (a - hi.astype(F32)).astype(BF16)
    return hi, lo


def _dot3(a, b):
    ah, al = _split_bf16(a)
    bh, bl = _split_bf16(b)
    d = functools.partial(jnp.dot, preferred_element_type=F32)
    return d(ah, bh) + d(ah, bl) + d(al, bh)


def _inproj_kernel(x_ref, g_ref, w_ref, gq_ref, gk_ref, s_ref, o_ref, h_scr, *, n_q_tiles):
    j = pl.program_id(1)

    @pl.when(j == 0)
    def _():
        x = x_ref[...]
        ms = jnp.mean(x * x, axis=-1, keepdims=True)
        h_scr[...] = (x * lax.rsqrt(ms + EPS) * g_ref[...]).astype(BF16)

    acc = jnp.dot(h_scr[...], w_ref[...], preferred_element_type=F32)
    tn = acc.shape[1]

    def head_norm(a, g):
        hi, lo = _split_bf16(a * a)
        s = s_ref[:a.shape[1], :a.shape[1]]
        ss = (jnp.dot(hi, s, preferred_element_type=F32)
              + jnp.dot(lo, s, preferred_element_type=F32))
        return a * lax.rsqrt(ss * (1.0 / HEAD_DIM) + EPS) * g

    @pl.when(j < n_q_tiles)
    def _():
        o_ref[...] = head_norm(acc, gq_ref[...]).astype(o_ref.dtype)

    @pl.when(j == n_q_tiles)
    def _():
        half = tn // 2
        o_ref[:, :half] = head_norm(acc[:, :half], gk_ref[...]).astype(o_ref.dtype)
        o_ref[:, half:] = acc[:, half:].astype(o_ref.dtype)

    @pl.when(j > n_q_tiles)
    def _():
        o_ref[...] = acc.astype(o_ref.dtype)


def _inproj(x2d, g_mix, w_in_bf, g_q, g_k, attn_w, kv_w):
    t, d = x2d.shape
    in_w = w_in_bf.shape[1]
    tn = 2 * kv_w
    tm = min(1024, t)
    assert attn_w % tn == 0 and in_w % tn == 0 and t % tm == 0
    n_q_tiles = attn_w // tn
    heads_per_tile = tn // HEAD_DIM
    blk = np.kron(np.eye(heads_per_tile, dtype=np.float32), np.ones((HEAD_DIM, HEAD_DIM), np.float32))
    gq_t = jnp.tile(g_q.reshape(1, HEAD_DIM), (1, heads_per_tile))
    gk_t = jnp.tile(g_k.reshape(1, HEAD_DIM), (1, heads_per_tile // 2))
    return pl.pallas_call(
        functools.partial(_inproj_kernel, n_q_tiles=n_q_tiles),
        grid=(t // tm, in_w // tn),
        in_specs=[
            pl.BlockSpec((tm, d), lambda i, j: (i, 0)),
            pl.BlockSpec((1, d), lambda i, j: (0, 0)),
            pl.BlockSpec((d, tn), lambda i, j: (0, j)),
            pl.BlockSpec((1, tn), lambda i, j: (0, 0)),
            pl.BlockSpec((1, tn // 2), lambda i, j: (0, 0)),
            pl.BlockSpec((tn, tn), lambda i, j: (0, 0)),
        ],
        out_specs=pl.BlockSpec((tm, tn), lambda i, j: (i, j)),
        out_shape=jax.ShapeDtypeStruct((t, in_w), BF16),
        scratch_shapes=[pltpu.VMEM((tm, d), BF16)],
        compiler_params=_cparams(("arbitrary", "arbitrary")),
        name="inproj",
    )(x2d, g_mix.reshape(1, d), w_in_bf, gq_t, gk_t, jnp.asarray(blk, BF16))


def _kv_start(i, tq, span, seq):
    return jnp.clip(i * tq - WINDOW, 0, seq - span)


def _attn_kernel(q_ref, k_ref, v_ref, bias_ref, sink_ref, g_ref, o_ref, o_scr, s_scr, p_scr,
                 *, n_heads, n_kv, tq, span, seq):
    i = pl.program_id(1)
    grp = n_heads // n_kv
    rows = 2 * SUBLANES
    start = pl.multiple_of(_kv_start(i, tq, span, seq), WINDOW)
    for kh in range(n_kv):
        kk = k_ref[0, pl.ds(start, span), kh * HEAD_DIM:(kh + 1) * HEAD_DIM] * (HEAD_DIM ** -0.5)
        for h in range(kh * grp, (kh + 1) * grp):
            q = q_ref[0, :, h * HEAD_DIM:(h + 1) * HEAD_DIM]
            s_scr[h * tq:(h + 1) * tq, :] = (lax.dot_general(q, kk, NT_DIMS, preferred_element_type=F32)
                                             + bias_ref[0, h * tq:(h + 1) * tq, :])
    for h in range(n_heads):
        sink = sink_ref[h]
        for r in range(h * tq, (h + 1) * tq, rows):
            s = s_scr[r:r + rows, :]
            m = jnp.maximum(jnp.max(s, axis=-1, keepdims=True), sink)
            p = jnp.exp(s - m)
            denom = jnp.sum(p, axis=-1, keepdims=True) + jnp.exp(sink - m)
            p_scr[r:r + rows, :] = (p / denom).astype(BF16)
    for kh in range(n_kv):
        vv = v_ref[0, pl.ds(start, span), kh * HEAD_DIM:(kh + 1) * HEAD_DIM]
        for h in range(kh * grp, (kh + 1) * grp):
            o_scr[:, h * HEAD_DIM:(h + 1) * HEAD_DIM] = jnp.dot(p_scr[h * tq:(h + 1) * tq, :], vv,
                                                                preferred_element_type=F32)
    a = o_scr[...]
    ms = jnp.mean(a * a, axis=-1, keepdims=True)
    o_ref[0] = (a * lax.rsqrt(ms + EPS) * g_ref[...]).astype(o_ref.dtype)


@functools.lru_cache(maxsize=None)
def _alibi_window_bias(n_heads, tq, span):
    slopes = np.exp2(-8.0 * np.arange(1, n_heads + 1, dtype=np.float32) / n_heads).astype(np.float32)
    r = np.arange(tq)[:, None]
    c = np.arange(span)[None, :]
    tables = []
    for v in range(3):
        dist = np.abs(r - c + v * WINDOW)
        bias = np.where(dist <= WINDOW, -slopes[:, None, None] * dist.astype(np.float32), -np.inf)
        tables.append(bias.reshape(n_heads * tq, span))
    return np.stack(tables).astype(np.float32)


def _attention(proj3, sinks, g_attn_out, n_heads, kv_w):
    b, seq, _ = proj3.shape
    attn_w = n_heads * HEAD_DIM
    n_kv = kv_w // HEAD_DIM
    tq = WINDOW
    span = min(3 * WINDOW, seq)
    kcol = attn_w // kv_w
    bias = jnp.asarray(_alibi_window_bias(n_heads, tq, span))
    return pl.pallas_call(
        functools.partial(_attn_kernel, n_heads=n_heads, n_kv=n_kv, tq=tq, span=span, seq=seq),
        grid=(b, seq // tq),
        in_specs=[
            pl.BlockSpec((1, tq, attn_w), lambda bi, i: (bi, i, 0)),
            pl.BlockSpec((1, seq, kv_w), lambda bi, i: (bi, 0, kcol)),
            pl.BlockSpec((1, seq, kv_w), lambda bi, i: (bi, 0, kcol + 1)),
            pl.BlockSpec((1, n_heads * tq, span),
                         lambda bi, i: ((i * tq - _kv_start(i, tq, span, seq)) // WINDOW, 0, 0)),
            pl.BlockSpec(memory_space=pltpu.SMEM),
            pl.BlockSpec((1, attn_w), lambda bi, i: (0, 0)),
        ],
        out_specs=pl.BlockSpec((1, tq, attn_w), lambda bi, i: (bi, i, 0)),
        out_shape=jax.ShapeDtypeStruct((b, seq, attn_w), BF16),
        scratch_shapes=[pltpu.VMEM((tq, attn_w), F32), pltpu.VMEM((n_heads * tq, span), F32),
                        pltpu.VMEM((n_heads * tq, span), BF16)],
        compiler_params=_cparams(("arbitrary", "arbitrary")),
        name="attention",
    )(proj3, proj3, proj3, bias, sinks.astype(F32), g_attn_out.reshape(1, attn_w))


def _fft_plan(seq):
    n = 2 * seq
    n2 = 1 << (int(math.log2(n)) // 2)
    n1 = n // n2
    return n, n1, n2


@functools.lru_cache(maxsize=None)
def _dft_constants(seq):
    n, n1, n2 = _fft_plan(seq)
    n1h = n1 // 2

    def blockc(re, im):
        return np.block([[re, -im], [im, re]])

    k1 = np.arange(n1)[:, None]
    a1 = 2.0 * np.pi * (k1 * np.arange(n1h)[None, :]) / n1
    f1 = blockc(np.cos(a1), -np.sin(a1))
    f1i = blockc(np.cos(a1.T), np.sin(a1.T))
    kk1 = np.arange(n1)[:, None, None]
    kk2 = np.arange(n2)[None, :, None]
    nn2 = np.arange(n2)[None, None, :]
    a2 = 2.0 * np.pi * (nn2 * (kk1 + n1 * kk2)) / n
    c2, s2 = np.cos(a2), np.sin(a2)
    g2 = np.concatenate([np.concatenate([c2, s2], axis=2),
                         np.concatenate([-s2, c2], axis=2)], axis=1)
    c2t, s2t = np.swapaxes(c2, 1, 2), np.swapaxes(s2, 1, 2)
    g2i = np.concatenate([np.concatenate([c2t, -s2t], axis=2),
                          np.concatenate([s2t, c2t], axis=2)], axis=1)
    return (f1.astype(np.float32), f1i.astype(np.float32),
            g2.astype(np.float32), g2i.astype(np.float32))


def _pack_complex(re, im):
    rb = lax.bitcast_convert_type(re.astype(BF16).astype(F32), jnp.uint32)
    ib = lax.bitcast_convert_type(im.astype(BF16).astype(F32), jnp.uint32)
    return (rb & jnp.uint32(0xFFFF0000)) | (ib >> 16)


def _unpack_complex(w):
    re = lax.bitcast_convert_type(w & jnp.uint32(0xFFFF0000), F32).astype(BF16)
    im = lax.bitcast_convert_type(w << 16, F32).astype(BF16)
    return re, im


def _fft_stage1(load_slab, f1_ref, a_scr, n1, n2):
    def body(s, carry):
        a = jnp.dot(f1_ref[...], load_slab(s), preferred_element_type=F32)
        a_scr[pl.ds(pl.multiple_of(s * _pitch(n1), SUBLANES), n1), :] = _pack_complex(a[:n1], a[n1:])
        return carry
    lax.fori_loop(0, n2, body, 0, unroll=FFT_UNROLL)


def _pitch(n1):
    return n1 + SUBLANES


def _load_freq_rows(a_scr, k1, n1, n2):
    re, im = _unpack_complex(a_scr[pl.ds(k1, n2, stride=_pitch(n1)), :])
    return jnp.concatenate([re, im], axis=0)


def _shortconv_kernel(p_ref, w_ref, b_ref, o_ref, u_scr, *, seq, n2, chunk):
    n1h = seq // n2
    up = _pitch(n2)
    w0, w1, w2 = w_ref[0:1, :], w_ref[1:2, :], w_ref[2:3, :]
    bias = b_ref[...]
    halo = 16
    for c in range(seq // chunk):
        r0 = c * chunk
        lo = max(r0 - halo, 0)
        hi = min(r0 + chunk + halo, seq)
        win = p_ref[0, lo:hi, :].astype(F32)
        off = r0 - lo
        cur = win[off:off + chunk]
        if r0 > 0:
            prev = win[off - 1:off - 1 + chunk]
        else:
            prev = jnp.concatenate([jnp.zeros((1, LANES), F32), win[0:chunk - 1]], axis=0)
        if r0 + chunk < seq:
            nxt = win[off + 1:off + 1 + chunk]
        else:
            nxt = jnp.concatenate([win[off + 1:off + chunk], jnp.zeros((1, LANES), F32)], axis=0)
        u = prev * w0 + cur * w1 + nxt * w2 + bias
        for gi in range(chunk // n2):
            r1 = (r0 // n2 + gi) * up
            u_scr[r1:r1 + n2, :] = u[gi * n2:(gi + 1) * n2]

    def body(s, carry):
        o_ref[0, s] = u_scr[pl.ds(s, n1h, stride=up), :].astype(o_ref.dtype)
        return carry
    lax.fori_loop(0, n2, body, 0, unroll=FFT_UNROLL)


def _shortconv(proj3, w_short, b_short, col0):
    b, seq, _ = proj3.shape
    cu = w_short.shape[1]
    _, n1, n2 = _fft_plan(seq)
    n1h = n1 // 2
    chunk = min(512, seq)
    cb0 = col0 // LANES
    return pl.pallas_call(
        functools.partial(_shortconv_kernel, seq=seq, n2=n2, chunk=chunk),
        grid=(b, cu // LANES),
        in_specs=[
            pl.BlockSpec((1, seq, LANES), lambda bi, c: (bi, 0, cb0 + c)),
            pl.BlockSpec((3, LANES), lambda bi, c: (0, c)),
            pl.BlockSpec((1, LANES), lambda bi, c: (0, c)),
        ],
        out_specs=pl.BlockSpec((1, n2, n1h, LANES), lambda bi, c: (bi, 0, 0, c)),
        out_shape=jax.ShapeDtypeStruct((b, n2, n1h, cu), BF16),
        scratch_shapes=[pltpu.VMEM((n1h * _pitch(n2), LANES), F32)],
        compiler_params=_cparams(("arbitrary", "arbitrary")),
        name="shortconv",
    )(proj3, w_short, b_short.reshape(1, cu))


def _filtergen_kernel(z_ref, w1_ref, b1_ref, f1_ref, w2_ref, b2_ref, f2_ref, w3_ref, b3_ref, f3_ref,
                      w4_ref, dl_ref, o_ref, h_scr):
    j = pl.program_id(1)

    @pl.when(j == 0)
    def _():
        z = z_ref[...]
        h = jnp.sin(f1_ref[...] * (_dot3(z, w1_ref[...]) + b1_ref[...]))
        h = jnp.sin(f2_ref[...] * (_dot3(h, w2_ref[...]) + b2_ref[...]))
        h_scr[...] = jnp.sin(f3_ref[...] * (_dot3(h, w3_ref[...]) + b3_ref[...]))

    t01 = z_ref[:, 0:1]
    decay = jnp.exp(-t01 * dl_ref[...])
    o_ref[...] = _dot3(h_scr[...], w4_ref[...]) * decay


@functools.lru_cache(maxsize=None)
def _filter_features(seq):
    _, n1, n2 = _fft_plan(seq)
    n1h = n1 // 2
    t01 = np.linspace(0.0, 1.0, seq, dtype=np.float32)[:, None]
    ang = (2.0 * math.pi * np.arange(seq, dtype=np.float32)[:, None] / seq).astype(np.float32)
    bands = np.linspace(1e-4, POS_BANDS - 1, POS_BANDS, dtype=np.float32)[None, :]
    arg = (bands * ang).astype(np.float32)
    z = np.concatenate([t01, np.cos(arg), -np.sin(arg)], axis=-1).astype(np.float32)
    pos = (np.arange(n1h)[None, :] * n2 + np.arange(n2)[:, None]).reshape(-1)
    zp = np.zeros((seq, FEAT_PAD), np.float32)
    zp[:, :z.shape[1]] = z[pos]
    return zp


def _filtergen(seq, w1, b1, f1, w2, b2, f2, w3, b3, f3, w4, hy_ch):
    fw = w1.shape[1]
    fo = w4.shape[1]
    z = jnp.asarray(_filter_features(seq))
    pe = z.shape[1]
    w1 = jnp.pad(w1, ((0, pe - w1.shape[0]), (0, 0)))
    deltas = np.abs(np.linspace(math.log(DECAY_FAST) / DECAY_TARGET, math.log(DECAY_SLOW) / DECAY_TARGET,
                                hy_ch, dtype=np.float32))
    dl = jnp.asarray(np.tile(deltas, fo // hy_ch).reshape(1, fo))
    tr = min(512, seq)
    tc = min(512, fo)
    row = lambda a: a.reshape(1, -1)
    full = lambda shp: pl.BlockSpec(shp, lambda i, j: (0, 0))
    return pl.pallas_call(
        _filtergen_kernel,
        grid=(seq // tr, fo // tc),
        in_specs=[
            pl.BlockSpec((tr, pe), lambda i, j: (i, 0)),
            full((pe, fw)), full((1, fw)), full((1, fw)),
            full((fw, fw)), full((1, fw)), full((1, fw)),
            full((fw, fw)), full((1, fw)), full((1, fw)),
            pl.BlockSpec((fw, tc), lambda i, j: (0, j)),
            pl.BlockSpec((1, tc), lambda i, j: (0, j)),
        ],
        out_specs=pl.BlockSpec((tr, tc), lambda i, j: (i, j)),
        out_shape=jax.ShapeDtypeStruct((seq, fo), F32),
        scratch_shapes=[pltpu.VMEM((tr, fw), F32)],
        compiler_params=_cparams(("arbitrary", "arbitrary")),
        name="filtergen",
    )(z, w1, row(b1), row(f1), w2, row(b2), row(f2), w3, row(b3), row(f3), w4, dl)


def _filterfft_kernel(hf_ref, hb_ref, f1_ref, g2_ref, o_ref, af_scr, ab_scr, *, n1, n2, k1c, scale):
    kc = pl.program_id(1)

    @pl.when(kc == 0)
    def _():
        _fft_stage1(lambda s: hf_ref[s].astype(BF16), f1_ref, af_scr, n1, n2)
        _fft_stage1(lambda s: hb_ref[s].astype(BF16), f1_ref, ab_scr, n1, n2)

    def body(kk, carry):
        k1 = kc * k1c + kk
        rows = jnp.concatenate([_load_freq_rows(af_scr, k1, n1, n2), _load_freq_rows(ab_scr, k1, n1, n2)], axis=1)
        x = jnp.dot(g2_ref[kk], rows, preferred_element_type=F32)
        xf, xb = x[:, :LANES], x[:, LANES:]
        o_ref[kk] = jnp.concatenate([xf[:n2] + xb[:n2], xf[n2:] - xb[n2:]], axis=0) * scale
        return carry
    lax.fori_loop(0, k1c, body, 0, unroll=min(k1c, STAGE2_UNROLL))


def _filter_spectrum(hs, seq, hy_ch):
    n, n1, n2 = _fft_plan(seq)
    n1h = n1 // 2
    cols = hs.shape[-1]
    cpt = hy_ch // LANES
    f1, _, g2, _ = _dft_constants(seq)
    f1r = jnp.asarray(f1[:, :n1h], BF16)
    g2b = jnp.asarray(g2, BF16)
    k1c = _k1_chunk(n1)
    hs3 = hs.reshape(n2, n1h, cols)
    scratch = pltpu.VMEM((n2 * _pitch(n1), LANES), jnp.uint32)
    return pl.pallas_call(
        functools.partial(_filterfft_kernel, n1=n1, n2=n2, k1c=k1c, scale=1.0 / n),
        grid=(2 * cpt, n1 // k1c),
        in_specs=[
            pl.BlockSpec((n2, n1h, LANES), lambda c, k: (0, 0, (c // cpt) * 2 * cpt + c % cpt)),
            pl.BlockSpec((n2, n1h, LANES), lambda c, k: (0, 0, (c // cpt) * 2 * cpt + cpt + c % cpt)),
            pl.BlockSpec((2 * n1, n1h), lambda c, k: (0, 0)),
            pl.BlockSpec((k1c, 2 * n2, 2 * n2), lambda c, k: (k, 0, 0)),
        ],
        out_specs=pl.BlockSpec((k1c, 2 * n2, LANES), lambda c, k: (k, 0, c)),
        out_shape=jax.ShapeDtypeStruct((n1, 2 * n2, 2 * hy_ch), F32),
        scratch_shapes=[scratch, scratch],
        compiler_params=_cparams(("arbitrary", "arbitrary")),
        name="filterfft",
    )(hs3, hs3, f1r, g2b)


def _k1_chunk(n1):
    return n1 if n1 <= 64 else 8


def _hyena_kernel(u_ref, g_ref, skip_ref, h_ref, f1_ref, f1i_ref, g2_ref, g2i_ref, o_ref, a_scr,
                  *z_scr, n1, n2, k1c):
    kc = pl.program_id(2)
    n1h = n1 // 2
    zp = _pitch(n1h)

    @pl.when(kc == 0)
    def _():
        _fft_stage1(lambda s: jnp.concatenate([u_ref[0, s], u_ref[1, s]], axis=0), f1_ref, a_scr, n1, n2)

    def body(kk, carry):
        k1 = kc * k1c + kk
        x = jnp.dot(g2_ref[kk], _load_freq_rows(a_scr, k1, n1, n2), preferred_element_type=F32)
        xr, xi = x[:n2], x[n2:]
        h = h_ref[kk]
        hr, hi = h[:n2], h[n2:]
        y = jnp.concatenate([xr * hr - xi * hi, xr * hi + xi * hr], axis=0).astype(BF16)
        bq = jnp.dot(g2i_ref[kk], y, preferred_element_type=F32)
        a_scr[pl.ds(k1, n2, stride=_pitch(n1)), :] = _pack_complex(bq[:n2], bq[n2:])
        return carry
    lax.fori_loop(0, k1c, body, 0, unroll=min(k1c, STAGE2_UNROLL))

    @pl.when(kc == pl.num_programs(2) - 1)
    def _():
        skip = skip_ref[...]

        def body3(s, carry):
            re, im = _unpack_complex(a_scr[pl.ds(pl.multiple_of(s * _pitch(n1), SUBLANES), n1), :])
            y = jnp.dot(f1i_ref[...], jnp.concatenate([re, im], axis=0), preferred_element_type=F32)
            for bb in range(2):
                yb = y[bb * n1h:(bb + 1) * n1h]
                z = g_ref[bb, s].astype(F32) * (yb + skip * u_ref[bb, s].astype(F32))
                if z_scr:
                    z_scr[0][bb, pl.ds(pl.multiple_of(s * zp, SUBLANES), n1h), :] = z
                else:
                    o_ref[bb, s] = z.astype(o_ref.dtype)
            return carry
        lax.fori_loop(0, n2, body3, 0, unroll=FFT_UNROLL)

        if z_scr:
            def body4(r, carry):
                for bb in range(2):
                    rows = z_scr[0][bb, pl.ds(r, n2, stride=zp), :]
                    o_ref[bb, pl.ds(pl.multiple_of(r * n2, n2), n2), :] = rows.astype(o_ref.dtype)
                return carry
            lax.fori_loop(0, n1h, body4, 0, unroll=FFT_UNROLL)


def _hyena_conv(u_arr, u_col, g_arr, g_col, skip, tf, order, seq, hy_ch, token_order):
    b = u_arr.shape[0]
    n, n1, n2 = _fft_plan(seq)
    n1h = n1 // 2
    f1, f1i, g2, g2i = _dft_constants(seq)
    k1c = _k1_chunk(n1)
    ucb, gcb = u_col // LANES, g_col // LANES
    hcb = order * hy_ch // LANES
    scratch = [pltpu.VMEM((n2 * _pitch(n1), LANES), jnp.uint32)]
    if token_order:
        out_spec = pl.BlockSpec((2, seq, LANES), lambda c, p, k: (p, 0, c))
        out_shape = jax.ShapeDtypeStruct((b, seq, hy_ch), BF16)
        scratch.append(pltpu.VMEM((2, n2 * _pitch(n1h), LANES), F32))
    else:
        out_spec = pl.BlockSpec((2, n2, n1h, LANES), lambda c, p, k: (p, 0, 0, c))
        out_shape = jax.ShapeDtypeStruct((b, n2, n1h, hy_ch), BF16)
    return pl.pallas_call(
        functools.partial(_hyena_kernel, n1=n1, n2=n2, k1c=k1c),
        grid=(hy_ch // LANES, b // 2, n1 // k1c),
        in_specs=[
            pl.BlockSpec((2, n2, n1h, LANES), lambda c, p, k: (p, 0, 0, ucb + c)),
            pl.BlockSpec((2, n2, n1h, LANES), lambda c, p, k: (p, 0, 0, gcb + c)),
            pl.BlockSpec((1, LANES), lambda c, p, k: (0, c)),
            pl.BlockSpec((k1c, 2 * n2, LANES), lambda c, p, k: (k, 0, hcb + c)),
            pl.BlockSpec((2 * n1, n1), lambda c, p, k: (0, 0)),
            pl.BlockSpec((n1, 2 * n1), lambda c, p, k: (0, 0)),
            pl.BlockSpec((k1c, 2 * n2, 2 * n2), lambda c, p, k: (k, 0, 0)),
            pl.BlockSpec((k1c, 2 * n2, 2 * n2), lambda c, p, k: (k, 0, 0)),
        ],
        out_specs=out_spec,
        out_shape=out_shape,
        scratch_shapes=scratch,
        compiler_params=_cparams(("arbitrary", "arbitrary", "arbitrary")),
        name="hyenaconv",
    )(u_arr, g_arr, skip.reshape(1, hy_ch), tf,
      jnp.asarray(f1, BF16), jnp.asarray(f1i, BF16), jnp.asarray(g2, BF16), jnp.asarray(g2i, BF16))


def _outproj_kernel(a_ref, z_ref, x_ref, wa_ref, wh_ref, gh_ref, gf_ref, wrh_ref, wrl_ref, br_ref, c0_ref,
                    tri_ref, x1_ref, xn_ref, idx_ref, gate_ref, rank_ref, cnt_ref, run_scr, *, n_exp):
    i = pl.program_id(0)

    @pl.when(i == 0)
    def _():
        run_scr[...] = c0_ref[...]

    z = z_ref[...].astype(F32)
    zn = (z * lax.rsqrt(jnp.mean(z * z, axis=-1, keepdims=True) + EPS) * gh_ref[...]).astype(BF16)
    mixed = (jnp.dot(a_ref[...], wa_ref[...], preferred_element_type=F32)
             + jnp.dot(zn, wh_ref[...], preferred_element_type=F32))
    x1 = x_ref[...] + mixed
    x1_ref[...] = x1
    xn = x1 * lax.rsqrt(jnp.mean(x1 * x1, axis=-1, keepdims=True) + EPS) * gf_ref[...]
    half = xn.shape[1] // 2
    packed = _pack_complex(xn[:, :half], xn[:, half:])
    for c in range(SUBLANES):
        xn_ref[pl.ds(c, xn.shape[0], stride=SUBLANES), :] = packed[:, c * LANES:(c + 1) * LANES]

    xh, xl = _split_bf16(xn)
    dg = functools.partial(lax.dot_general, dimension_numbers=NT_DIMS, preferred_element_type=F32)
    logit = dg(wrh_ref[...], xh) + dg(wrh_ref[...], xl) + dg(wrl_ref[...], xh) + br_ref[...]
    tm = logit.shape[1]
    eio = lax.broadcasted_iota(jnp.int32, (n_exp, tm), 0)
    vals, onehots = [], []
    cur = logit
    for k in range(TOP_K):
        m = jnp.max(cur, axis=0, keepdims=True)
        idx = jnp.min(jnp.where(cur == m, eio, n_exp), axis=0, keepdims=True)
        oh = eio == idx
        vals.append(m)
        onehots.append(oh)
        idx_ref[k:k + 1, :] = idx
        cur = jnp.where(oh, -jnp.inf, cur)
    ex = [jnp.exp(v - vals[0]) for v in vals]
    tot = ex[0] + ex[1] + ex[2] + ex[3]
    for k in range(TOP_K):
        gate_ref[k:k + 1, :] = ex[k] / tot
    ohsum = jnp.zeros((n_exp, tm), F32)
    for oh in onehots:
        ohsum = ohsum + oh.astype(F32)
    base = jnp.dot(ohsum.astype(BF16), tri_ref[...], preferred_element_type=F32) + run_scr[...]
    for k in range(TOP_K):
        rank_ref[k:k + 1, :] = jnp.sum(jnp.where(onehots[k], base, 0.0), axis=0, keepdims=True).astype(jnp.int32)
    run_scr[...] = run_scr[...] + jnp.sum(ohsum, axis=1, keepdims=True)
    cnt_ref[...] = run_scr[...]


def _outproj_router(attn_n, z2, x2d, wa, wh, g_hy, g_ffn, wr_hi, wr_lo, b_router, count0):
    t, d = x2d.shape
    aw, hc = attn_n.shape[1], z2.shape[1]
    n_exp = wr_hi.shape[0]
    tm = min(512, t)
    tri = jnp.asarray(np.triu(np.ones((tm, tm), np.float32), k=1), BF16)
    full = lambda shp: pl.BlockSpec(shp, lambda i: (0, 0))
    resident = lambda shp: pl.BlockSpec(shp, lambda i: (0, 0), pipeline_mode=pl.Buffered(1))
    rowblk = lambda w: pl.BlockSpec((tm, w), lambda i: (i, 0))
    colblk = pl.BlockSpec((TOP_K, tm), lambda i: (0, i))
    return pl.pallas_call(
        functools.partial(_outproj_kernel, n_exp=n_exp),
        grid=(t // tm,),
        in_specs=[rowblk(aw), rowblk(hc), rowblk(d), resident((aw, d)), resident((hc, d)), full((1, hc)),
                  full((1, d)), full((n_exp, d)), full((n_exp, d)), full((n_exp, 1)), full((n_exp, 1)),
                  resident((tm, tm))],
        out_specs=[rowblk(d), pl.BlockSpec((tm * SUBLANES, LANES), lambda i: (i, 0)),
                   colblk, colblk, colblk, full((n_exp, 1))],
        out_shape=[jax.ShapeDtypeStruct((t, d), F32), jax.ShapeDtypeStruct((t * SUBLANES, LANES), jnp.uint32),
                   jax.ShapeDtypeStruct((TOP_K, t), jnp.int32), jax.ShapeDtypeStruct((TOP_K, t), F32),
                   jax.ShapeDtypeStruct((TOP_K, t), jnp.int32), jax.ShapeDtypeStruct((n_exp, 1), F32)],
        scratch_shapes=[pltpu.VMEM((n_exp, 1), F32)],
        compiler_params=_cparams(("arbitrary",)),
        name="outproj_router",
    )(attn_n, z2, x2d, wa, wh, g_hy.reshape(1, hc), g_ffn.reshape(1, d), wr_hi, wr_lo,
      b_router.reshape(n_exp, 1), count0, tri)


def _dest_kernel(ps_ref, idx_ref, rank_ref, o_ref, *, n_exp):
    idx = idx_ref[...]
    acc = rank_ref[...]
    for e in range(n_exp):
        acc = acc + jnp.where(idx == e, ps_ref[e], 0)
    o_ref[...] = acc


def _dest_rows(pstart, idx, rank):
    k, t = idx.shape
    tb = min(2048, t)
    blk = pl.BlockSpec((k, tb), lambda i: (0, i))
    return pl.pallas_call(
        functools.partial(_dest_kernel, n_exp=pstart.shape[0]),
        grid=(t // tb,),
        in_specs=[pl.BlockSpec(memory_space=pltpu.SMEM), blk, blk],
        out_specs=blk,
        out_shape=jax.ShapeDtypeStruct((k, t), jnp.int32),
        compiler_params=_cparams(("arbitrary",)),
        name="dest",
    )(pstart, idx, rank)


def _dispatch_kernel(dest_ref, xn_ref, xs_in_ref, xs_ref, sem):
    del xs_in_ref
    tm = xn_ref.shape[0] // SUBLANES

    def row_copy(t, d):
        src = xn_ref.at[pl.ds(pl.multiple_of(t * SUBLANES, SUBLANES), SUBLANES)]
        return pltpu.make_async_copy(src, xs_ref.at[d], sem)

    def issue(t, carry):
        for k in range(TOP_K):
            row_copy(t, dest_ref[k, t]).start(priority=k % 2)
        return carry
    lax.fori_loop(0, tm, issue, 0)

    def drain(t, carry):
        for k in range(TOP_K):
            row_copy(0, 0).wait()
        return carry
    lax.fori_loop(0, tm, drain, 0)


def _dispatch(dest, xn, xs):
    t = dest.shape[1]
    tm = min(256, t)
    return pl.pallas_call(
        _dispatch_kernel,
        grid=(t // tm,),
        in_specs=[pl.BlockSpec((TOP_K, tm), lambda i: (0, i), memory_space=pltpu.SMEM),
                  pl.BlockSpec((tm * SUBLANES, LANES), lambda i: (i, 0)),
                  pl.BlockSpec(memory_space=pl.ANY)],
        out_specs=pl.BlockSpec(memory_space=pl.ANY),
        out_shape=jax.ShapeDtypeStruct(xs.shape, xs.dtype),
        scratch_shapes=[pltpu.SemaphoreType.DMA(())],
        input_output_aliases={2: 0},
        compiler_params=_cparams(("arbitrary",)),
        name="dispatch",
    )(dest, xn, xs)


def _expert_kernel(be_ref, nu_ref, x_ref, wgu_ref, bgu_ref, wd_ref, bd_ref, y_ref, x_scr, *, tf):
    del be_ref
    bm, d = x_scr.shape
    half = d // 2
    f = wd_ref.shape[1]

    @pl.when(pl.program_id(0) < nu_ref[0])
    def _():
        for c in range(SUBLANES):
            hi, lo = _unpack_complex(x_ref[pl.ds(c, bm, stride=SUBLANES), :])
            x_scr[:, c * LANES:(c + 1) * LANES] = hi
            x_scr[:, half + c * LANES:half + (c + 1) * LANES] = lo
        x = x_scr[...]
        acc = bd_ref[0]
        for c0 in range(0, f, tf):
            g = jnp.dot(x, wgu_ref[0, :, c0:c0 + tf], preferred_element_type=F32) + bgu_ref[0, :, c0:c0 + tf]
            u = (jnp.dot(x, wgu_ref[0, :, f + c0:f + c0 + tf], preferred_element_type=F32)
                 + bgu_ref[0, :, f + c0:f + c0 + tf])
            glu = jnp.minimum(g, SWIGLU_LIMIT)
            lin = jnp.clip(u, -SWIGLU_LIMIT, SWIGLU_LIMIT)
            h = glu * jax.nn.sigmoid(SWIGLU_ALPHA * glu) * (lin + 1.0)
            acc = acc + jnp.dot(h.astype(BF16), wd_ref[0, c0:c0 + tf, :], preferred_element_type=F32)
        for c in range(SUBLANES):
            y_ref[pl.ds(c, bm, stride=SUBLANES), :] = _pack_complex(
                acc[:, c * LANES:(c + 1) * LANES], acc[:, half + c * LANES:half + (c + 1) * LANES])


def _experts(block_expert, n_used, xs, w_gu, b_gu, w_dn, b_dn):
    n_exp, d, f2 = w_gu.shape
    n_rows = xs.shape[0] // SUBLANES
    f = f2 // 2
    tf = next(c for c in (512, 256, 128) if f % c == 0)
    bm = MOE_ROWS
    last = lambda i, nu: jnp.minimum(i, nu[0] - 1)
    resident = pl.Buffered(1)
    grid_spec = pltpu.PrefetchScalarGridSpec(
        num_scalar_prefetch=2,
        grid=(n_rows // bm,),
        in_specs=[
            pl.BlockSpec((bm * SUBLANES, LANES), lambda i, be, nu: (last(i, nu), 0)),
            pl.BlockSpec((1, d, f2), lambda i, be, nu: (be[i], 0, 0), pipeline_mode=resident),
            pl.BlockSpec((1, 1, f2), lambda i, be, nu: (be[i], 0, 0)),
            pl.BlockSpec((1, f, d), lambda i, be, nu: (be[i], 0, 0), pipeline_mode=resident),
            pl.BlockSpec((1, 1, d), lambda i, be, nu: (be[i], 0, 0)),
        ],
        out_specs=pl.BlockSpec((bm * SUBLANES, LANES), lambda i, be, nu: (last(i, nu), 0)),
        scratch_shapes=[pltpu.VMEM((bm, d), BF16)],
    )
    return pl.pallas_call(
        functools.partial(_expert_kernel, tf=tf),
        grid_spec=grid_spec,
        out_shape=jax.ShapeDtypeStruct((n_rows * SUBLANES, LANES), jnp.uint32),
        compiler_params=_cparams(("arbitrary",)),
        name="experts",
    )(block_expert, n_used, xs, w_gu, b_gu.reshape(n_exp, 1, f2), w_dn, b_dn.reshape(n_exp, 1, d))


def _combine_kernel(dest_ref, dest_next_ref, x1_ref, gate_ref, ys_ref, o_ref, buf, sems):
    i = pl.program_id(0)
    tm, d = x1_ref.shape
    half = d // 2
    slot = lax.rem(i, 2)

    def row_copy(s, t, k, d_row):
        dst = buf.at[s, k, pl.ds(pl.multiple_of(t * SUBLANES, SUBLANES), SUBLANES)]
        return pltpu.make_async_copy(ys_ref.at[d_row], dst, sems.at[s])

    def issue_tile(d_ref, s):
        def issue(t, carry):
            for k in range(TOP_K):
                row_copy(s, t, k, d_ref[k, t]).start(priority=k % 2)
            return carry
        lax.fori_loop(0, tm, issue, 0)

    @pl.when(i == 0)
    def _():
        issue_tile(dest_ref, 0)

    @pl.when(i + 1 < pl.num_programs(0))
    def _():
        issue_tile(dest_next_ref, 1 - slot)

    def drain(t, carry):
        for k in range(TOP_K):
            row_copy(slot, 0, 0, 0).wait()
        return carry
    lax.fori_loop(0, tm, drain, 0)

    gate = gate_ref[...]
    for c in range(SUBLANES):
        lo_cols = slice(c * LANES, (c + 1) * LANES)
        hi_cols = slice(half + c * LANES, half + (c + 1) * LANES)
        moe_a = x1_ref[:, lo_cols]
        moe_b = x1_ref[:, hi_cols]
        for k in range(TOP_K):
            w = buf[slot, k, pl.ds(c, tm, stride=SUBLANES), :]
            g = gate[:, k:k + 1]
            moe_a = moe_a + g * lax.bitcast_convert_type(w & jnp.uint32(0xFFFF0000), F32)
            moe_b = moe_b + g * lax.bitcast_convert_type(w << 16, F32)
        o_ref[:, lo_cols] = moe_a
        o_ref[:, hi_cols] = moe_b


def _combine(dest, x1, gate_tk, ys):
    t, d = x1.shape
    tm = min(256, t)
    nt = t // tm
    return pl.pallas_call(
        _combine_kernel,
        grid=(nt,),
        in_specs=[pl.BlockSpec((TOP_K, tm), lambda i: (0, i), memory_space=pltpu.SMEM),
                  pl.BlockSpec((TOP_K, tm), lambda i: (0, jnp.minimum(i + 1, nt - 1)), memory_space=pltpu.SMEM),
                  pl.BlockSpec((tm, d), lambda i: (i, 0)),
                  pl.BlockSpec((tm, TOP_K), lambda i: (i, 0)),
                  pl.BlockSpec(memory_space=pl.ANY)],
        out_specs=pl.BlockSpec((tm, d), lambda i: (i, 0)),
        out_shape=jax.ShapeDtypeStruct((t, d), F32),
        scratch_shapes=[pltpu.VMEM((2, TOP_K, tm * SUBLANES, LANES), jnp.uint32), pltpu.SemaphoreType.DMA((2,))],
        compiler_params=_cparams(("arbitrary",)),
        name="combine",
    )(dest, dest, x1, gate_tk, ys)


def _mixer(x3, g_mix, w_in_bf, g_q, g_k, sinks, w_short, b_short, flt, hyena_skip, g_attn_out, dims):
    n_heads, kv_w, hy_ch = dims
    b, seq, d = x3.shape
    attn_w = n_heads * HEAD_DIM
    proj = _inproj(x3.reshape(b * seq, d), g_mix, w_in_bf, g_q, g_k, attn_w, kv_w)
    proj3 = proj.reshape(b, seq, proj.shape[1])
    attn_n = _attention(proj3, sinks, g_attn_out, n_heads, kv_w)
    u = _shortconv(proj3, w_short, b_short, attn_w + 2 * kv_w)
    hs = _filtergen(seq, *flt, hy_ch)
    tf = _filter_spectrum(hs, seq, hy_ch)
    z1 = _hyena_conv(u, 0, u, hy_ch, hyena_skip[0], tf, 0, seq, hy_ch, token_order=False)
    z2 = _hyena_conv(z1, 0, u, 2 * hy_ch, hyena_skip[1], tf, 1, seq, hy_ch, token_order=True)
    return attn_n.reshape(b * seq, attn_w), z2.reshape(b * seq, hy_ch)


def kernel(x_prompt, x_sample, g_mix, w_in, g_q, g_k, attn_sinks, w_short, b_short, flt_w1, flt_b1, flt_freq1, flt_w2, flt_b2, flt_freq2, flt_w3, flt_b3, flt_freq3, flt_w4, hyena_skip, g_attn_out, g_hyena_out, w_out, g_ffn, w_router, b_router, w_gate_up, b_gate_up, w_down, b_down):
    assert g_mix.shape[0] == 1, "single-layer trunk"
    d = x_prompt.shape[-1]
    assert d == 2 * SUBLANES * LANES, "routed-row tile layout is built for d_model = 2048"
    n_heads = attn_sinks.shape[-1]
    hy_ch = hyena_skip.shape[-1]
    attn_w = n_heads * HEAD_DIM
    kv_w = (w_in.shape[-1] - attn_w - 3 * hy_ch) // 2
    n_exp = w_router.shape[-1]
    dims = (n_heads, kv_w, hy_ch)

    w_in_bf = w_in[0].astype(BF16)
    wa = w_out[0, :attn_w].astype(BF16)
    wh = w_out[0, attn_w:].astype(BF16)
    wr_t = w_router[0].T
    wr_hi = wr_t.astype(BF16)
    wr_lo = (wr_t - wr_hi.astype(F32)).astype(BF16)
    w_gu = w_gate_up[0].astype(BF16)
    w_dn = w_down[0].astype(BF16)
    flt = (flt_w1[0], flt_b1[0], flt_freq1[0], flt_w2[0], flt_b2[0], flt_freq2[0],
           flt_w3[0], flt_b3[0], flt_freq3[0], flt_w4[0])

    xs_in = (x_prompt, x_sample)
    routed = []
    counts = jnp.zeros((n_exp, 1), F32)
    for x3 in xs_in:
        attn_n, z2 = _mixer(x3, g_mix[0], w_in_bf, g_q[0], g_k[0], attn_sinks[0], w_short[0], b_short[0],
                            flt, hyena_skip[0], g_attn_out[0], dims)
        x1, xn, idx, gate, rank, counts = _outproj_router(
            attn_n, z2, x3.reshape(-1, d), wa, wh, g_hyena_out[0], g_ffn[0], wr_hi, wr_lo, b_router[0], counts)
        routed.append((x1, xn, idx, gate, rank))

    n_slots = sum(r[0].shape[0] for r in routed) * TOP_K
    n_rows = (n_slots // MOE_ROWS + n_exp) * MOE_ROWS
    cnt = counts[:, 0].astype(jnp.int32)
    padded = (cnt + MOE_ROWS - 1) // MOE_ROWS * MOE_ROWS
    pend = jnp.cumsum(padded)
    pstart = (pend - padded).astype(jnp.int32)
    n_blocks = n_rows // MOE_ROWS
    block_row0 = jnp.arange(n_blocks, dtype=jnp.int32) * MOE_ROWS
    block_expert = jnp.minimum(jnp.sum(pend[None, :] <= block_row0[:, None], axis=1), n_exp - 1).astype(jnp.int32)
    n_used = (pend[-1:] // MOE_ROWS).astype(jnp.int32)

    xs = jnp.zeros((n_rows, SUBLANES, LANES), jnp.uint32)
    dests = []
    for (x1, xn, idx, gate, rank) in routed:
        dest = _dest_rows(pstart, idx, rank)
        xs = _dispatch(dest, xn, xs)
        dests.append(dest)
    ys = _experts(block_expert, n_used, xs.reshape(n_rows * SUBLANES, LANES), w_gu, b_gate_up[0], w_dn, b_down[0])
    ys = ys.reshape(n_rows, SUBLANES, LANES)
    outs = []
    for (x1, xn, idx, gate, rank), dest, x3 in zip(routed, dests, xs_in):
        outs.append(_combine(dest, x1, gate.T, ys).reshape(x3.shape))
    return tuple(outs)
```

```python
import functools
import math

import numpy as np
import jax
import jax.numpy as jnp
from jax import lax
from jax.experimental import pallas as pl
from jax.experimental.pallas import tpu as pltpu

HEAD_DIM = 64
WINDOW = 128
TOP_K = 4
EPS = 1e-6
POS_BANDS = 16
FEAT_PAD = 64
DECAY_FAST = 0.3
DECAY_SLOW = 1.5
DECAY_TARGET = 1e-2
SWIGLU_ALPHA = 1.702
SWIGLU_LIMIT = 7.0

LANES = 128
SUBLANES = 8
VMEM_LIMIT = 56 * 1024 * 1024
MOE_ROWS = 512
FFT_UNROLL = 8
STAGE2_UNROLL = 8
F32 = jnp.float32
BF16 = jnp.bfloat16
NT_DIMS = (((1,), (1,)), ((), ()))


def _cparams(sem):
    return pltpu.CompilerParams(dimension_semantics=sem, vmem_limit_bytes=VMEM_LIMIT)


def _split_bf16(a):
    hi = a.astype(BF16)
    lo = (a - hi.astype(F32)).astype(BF16)
    return hi, lo


def _dot3(a, b):
    ah, al = _split_bf16(a)
    bh, bl = _split_bf16(b)
    d = functools.partial(jnp.dot, preferred_element_type=F32)
    return d(ah, bh) + d(ah, bl) + d(al, bh)


def _inproj_kernel(x_ref, g_ref, w_ref, gq_ref, gk_ref, s_ref, o_ref, *, n_q_tiles, tn):
    x = x_ref[...]
    ms = jnp.mean(x * x, axis=-1, keepdims=True)
    h = (x * lax.rsqrt(ms + EPS) * g_ref[...]).astype(BF16)

    def head_norm(a, g):
        hi, lo = _split_bf16(a * a)
        s = s_ref[:a.shape[1], :a.shape[1]]
        ss = (jnp.dot(hi, s, preferred_element_type=F32)
              + jnp.dot(lo, s, preferred_element_type=F32))
        return a * lax.rsqrt(ss * (1.0 / HEAD_DIM) + EPS) * g

    for j in range(w_ref.shape[1] // tn):
        cols = slice(j * tn, (j + 1) * tn)
        acc = jnp.dot(h, w_ref[:, cols], preferred_element_type=F32)
        if j < n_q_tiles:
            o_ref[:, cols] = head_norm(acc, gq_ref[...]).astype(o_ref.dtype)
        elif j == n_q_tiles:
            half = tn // 2
            o_ref[:, j * tn:j * tn + half] = head_norm(acc[:, :half], gk_ref[...]).astype(o_ref.dtype)
            o_ref[:, j * tn + half:(j + 1) * tn] = acc[:, half:].astype(o_ref.dtype)
        else:
            o_ref[:, cols] = acc.astype(o_ref.dtype)


def _inproj(x2d, g_mix, w_in_bf, g_q, g_k, attn_w, kv_w):
    t, d = x2d.shape
    in_w = w_in_bf.shape[1]
    tn = 2 * kv_w
    tm = min(512, t)
    assert attn_w % tn == 0 and in_w % tn == 0 and t % tm == 0
    n_q_tiles = attn_w // tn
    heads_per_tile = tn // HEAD_DIM
    blk = np.kron(np.eye(heads_per_tile, dtype=np.float32), np.ones((HEAD_DIM, HEAD_DIM), np.float32))
    gq_t = jnp.tile(g_q.reshape(1, HEAD_DIM), (1, heads_per_tile))
    gk_t = jnp.tile(g_k.reshape(1, HEAD_DIM), (1, heads_per_tile // 2))
    return pl.pallas_call(
        functools.partial(_inproj_kernel, n_q_tiles=n_q_tiles, tn=tn),
        grid=(t // tm,),
        in_specs=[
            pl.BlockSpec((tm, d), lambda i: (i, 0)),
            pl.BlockSpec((1, d), lambda i: (0, 0)),
            pl.BlockSpec((d, in_w), lambda i: (0, 0), pipeline_mode=pl.Buffered(1)),
            pl.BlockSpec((1, tn), lambda i: (0, 0)),
            pl.BlockSpec((1, tn // 2), lambda i: (0, 0)),
            pl.BlockSpec((tn, tn), lambda i: (0, 0)),
        ],
        out_specs=pl.BlockSpec((tm, in_w), lambda i: (i, 0)),
        out_shape=jax.ShapeDtypeStruct((t, in_w), BF16),
        compiler_params=_cparams(("arbitrary",)),
        name="inproj",
    )(x2d, g_mix.reshape(1, d), w_in_bf, gq_t, gk_t, jnp.asarray(blk, BF16))


def _kv_start(i, tq, span, seq):
    return jnp.clip(i * tq - WINDOW, 0, seq - span)


def _attn_kernel(q_ref, k_ref, v_ref, bias_ref, sink_ref, g_ref, o_ref, o_scr, s_scr, p_scr,
                 *, n_heads, n_kv, tq, span, seq):
    i = pl.program_id(1)
    grp = n_heads // n_kv
    rows = 2 * SUBLANES
    start = pl.multiple_of(_kv_start(i, tq, span, seq), WINDOW)
    for kh in range(n_kv):
        kk = k_ref[0, pl.ds(start, span), kh * HEAD_DIM:(kh + 1) * HEAD_DIM] * (HEAD_DIM ** -0.5)
        for h in range(kh * grp, (kh + 1) * grp):
            q = q_ref[0, :, h * HEAD_DIM:(h + 1) * HEAD_DIM]
            s_scr[h * tq:(h + 1) * tq, :] = (lax.dot_general(q, kk, NT_DIMS, preferred_element_type=F32)
                                             + bias_ref[0, h * tq:(h + 1) * tq, :])
    for h in range(n_heads):
        sink = sink_ref[h]
        for r in range(h * tq, (h + 1) * tq, rows):
            s = s_scr[r:r + rows, :]
            m = jnp.maximum(jnp.max(s, axis=-1, keepdims=True), sink)
            p = jnp.exp(s - m)
            denom = jnp.sum(p, axis=-1, keepdims=True) + jnp.exp(sink - m)
            p_scr[r:r + rows, :] = (p / denom).astype(BF16)
    for kh in range(n_kv):
        vv = v_ref[0, pl.ds(start, span), kh * HEAD_DIM:(kh + 1) * HEAD_DIM]
        for h in range(kh * grp, (kh + 1) * grp):
            o_scr[:, h * HEAD_DIM:(h + 1) * HEAD_DIM] = jnp.dot(p_scr[h * tq:(h + 1) * tq, :], vv,
                                                                preferred_element_type=F32)
    a = o_scr[...]
    ms = jnp.mean(a * a, axis=-1, keepdims=True)
    o_ref[0] = (a * lax.rsqrt(ms + EPS) * g_ref[...]).astype(o_ref.dtype)


@functools.lru_cache(maxsize=None)
def _alibi_window_bias(n_heads, tq, span):
    slopes = np.exp2(-8.0 * np.arange(1, n_heads + 1, dtype=np.float32) / n_heads).astype(np.float32)
    r = np.arange(tq)[:, None]
    c = np.arange(span)[None, :]
    tables = []
    for v in range(3):
        dist = np.abs(r - c + v * WINDOW)
        bias = np.where(dist <= WINDOW, -slopes[:, None, None] * dist.astype(np.float32), -np.inf)
        tables.append(bias.reshape(n_heads * tq, span))
    return np.stack(tables).astype(np.float32)


def _attention(proj3, sinks, g_attn_out, n_heads, kv_w):
    b, seq, _ = proj3.shape
    attn_w = n_heads * HEAD_DIM
    n_kv = kv_w // HEAD_DIM
    tq = WINDOW
    span = min(3 * WINDOW, seq)
    kcol = attn_w // kv_w
    bias = jnp.asarray(_alibi_window_bias(n_heads, tq, span))
    return pl.pallas_call(
        functools.partial(_attn_kernel, n_heads=n_heads, n_kv=n_kv, tq=tq, span=span, seq=seq),
        grid=(b, seq // tq),
        in_specs=[
            pl.BlockSpec((1, tq, attn_w), lambda bi, i: (bi, i, 0)),
            pl.BlockSpec((1, seq, kv_w), lambda bi, i: (bi, 0, kcol)),
            pl.BlockSpec((1, seq, kv_w), lambda bi, i: (bi, 0, kcol + 1)),
            pl.BlockSpec((1, n_heads * tq, span),
                         lambda bi, i: ((i * tq - _kv_start(i, tq, span, seq)) // WINDOW, 0, 0)),
            pl.BlockSpec(memory_space=pltpu.SMEM),
            pl.BlockSpec((1, attn_w), lambda bi, i: (0, 0)),
        ],
        out_specs=pl.BlockSpec((1, tq, attn_w), lambda bi, i: (bi, i, 0)),
        out_shape=jax.ShapeDtypeStruct((b, seq, attn_w), BF16),
        scratch_shapes=[pltpu.VMEM((tq, attn_w), F32), pltpu.VMEM((n_heads * tq, span), F32),
                        pltpu.VMEM((n_heads * tq, span), BF16)],
        compiler_params=_cparams(("arbitrary", "arbitrary")),
        name="attention",
    )(proj3, proj3, proj3, bias, sinks.astype(F32), g_attn_out.reshape(1, attn_w))


def _fft_plan(seq):
    n = 2 * seq
    n2 = 1 << (int(math.log2(n)) // 2)
    n1 = n // n2
    return n, n1, n2


@functools.lru_cache(maxsize=None)
def _dft_constants(seq):
    n, n1, n2 = _fft_plan(seq)
    n1h = n1 // 2

    def blockc(re, im):
        return np.block([[re, -im], [im, re]])

    k1 = np.arange(n1)[:, None]
    a1 = 2.0 * np.pi * (k1 * np.arange(n1h)[None, :]) / n1
    f1 = blockc(np.cos(a1), -np.sin(a1))
    f1i = blockc(np.cos(a1.T), np.sin(a1.T))
    kk1 = np.arange(n1)[:, None, None]
    kk2 = np.arange(n2)[None, :, None]
    nn2 = np.arange(n2)[None, None, :]
    a2 = 2.0 * np.pi * (nn2 * (kk1 + n1 * kk2)) / n
    c2, s2 = np.cos(a2), np.sin(a2)
    g2 = np.concatenate([np.concatenate([c2, s2], axis=2),
                         np.concatenate([-s2, c2], axis=2)], axis=1)
    c2t, s2t = np.swapaxes(c2, 1, 2), np.swapaxes(s2, 1, 2)
    g2i = np.concatenate([np.concatenate([c2t, -s2t], axis=2),
                          np.concatenate([s2t, c2t], axis=2)], axis=1)
    return (f1.astype(np.float32), f1i.astype(np.float32),
            g2.astype(np.float32), g2i.astype(np.float32))


def _pack_complex(re, im):
    rb = lax.bitcast_convert_type(re.astype(BF16).astype(F32), jnp.uint32)
    ib = lax.bitcast_convert_type(im.astype(BF16).astype(F32), jnp.uint32)
    return (rb & jnp.uint32(0xFFFF0000)) | (ib >> 16)


def _unpack_complex(w):
    re = lax.bitcast_convert_type(w & jnp.uint32(0xFFFF0000), F32).astype(BF16)
    im = lax.bitcast_convert_type(w << 16, F32).astype(BF16)
    return re, im


def _fft_stage1(load_slab, f1_ref, a_scr, n1, n2):
    def body(s, carry):
        a = jnp.dot(f1_ref[...], load_slab(s), preferred_element_type=F32)
        a_scr[pl.ds(pl.multiple_of(s * _pitch(n1), SUBLANES), n1), :] = _pack_complex(a[:n1], a[n1:])
        return carry
    lax.fori_loop(0, n2, body, 0, unroll=FFT_UNROLL)


def _pitch(n1):
    return n1 + SUBLANES


def _load_freq_rows(a_scr, k1, n1, n2):
    re, im = _unpack_complex(a_scr[pl.ds(k1, n2, stride=_pitch(n1)), :])
    return jnp.concatenate([re, im], axis=0)


def _shortconv_kernel(p_ref, w_ref, b_ref, o_ref, u_scr, *, seq, n2, chunk):
    n1h = seq // n2
    up = _pitch(n2)
    w0, w1, w2 = w_ref[0:1, :], w_ref[1:2, :], w_ref[2:3, :]
    bias = b_ref[...]
    halo = 16
    for c in range(seq // chunk):
        r0 = c * chunk
        lo = max(r0 - halo, 0)
        hi = min(r0 + chunk + halo, seq)
        win = p_ref[0, lo:hi, :].astype(F32)
        off = r0 - lo
        cur = win[off:off + chunk]
        if r0 > 0:
            prev = win[off - 1:off - 1 + chunk]
        else:
            prev = jnp.concatenate([jnp.zeros((1, LANES), F32), win[0:chunk - 1]], axis=0)
        if r0 + chunk < seq:
            nxt = win[off + 1:off + 1 + chunk]
        else:
            nxt = jnp.concatenate([win[off + 1:off + chunk], jnp.zeros((1, LANES), F32)], axis=0)
        u = prev * w0 + cur * w1 + nxt * w2 + bias
        for gi in range(chunk // n2):
            r1 = (r0 // n2 + gi) * up
            u_scr[r1:r1 + n2, :] = u[gi * n2:(gi + 1) * n2]

    def body(s, carry):
        o_ref[0, s] = u_scr[pl.ds(s, n1h, stride=up), :].astype(o_ref.dtype)
        return carry
    lax.fori_loop(0, n2, body, 0, unroll=FFT_UNROLL)


def _shortconv(proj3, w_short, b_short, col0):
    b, seq, _ = proj3.shape
    cu = w_short.shape[1]
    _, n1, n2 = _fft_plan(seq)
    n1h = n1 // 2
    chunk = min(512, seq)
    cb0 = col0 // LANES
    return pl.pallas_call(
        functools.partial(_shortconv_kernel, seq=seq, n2=n2, chunk=chunk),
        grid=(b, cu // LANES),
        in_specs=[
            pl.BlockSpec((1, seq, LANES), lambda bi, c: (bi, 0, cb0 + c)),
            pl.BlockSpec((3, LANES), lambda bi, c: (0, c)),
            pl.BlockSpec((1, LANES), lambda bi, c: (0, c)),
        ],
        out_specs=pl.BlockSpec((1, n2, n1h, LANES), lambda bi, c: (bi, 0, 0, c)),
        out_shape=jax.ShapeDtypeStruct((b, n2, n1h, cu), BF16),
        scratch_shapes=[pltpu.VMEM((n1h * _pitch(n2), LANES), F32)],
        compiler_params=_cparams(("arbitrary", "arbitrary")),
        name="shortconv",
    )(proj3, w_short, b_short.reshape(1, cu))


def _filtergen_kernel(z_ref, w1_ref, b1_ref, f1_ref, w2_ref, b2_ref, f2_ref, w3_ref, b3_ref, f3_ref,
                      w4_ref, dl_ref, o_ref, h_scr):
    j = pl.program_id(1)

    @pl.when(j == 0)
    def _():
        z = z_ref[...]
        h = jnp.sin(f1_ref[...] * (_dot3(z, w1_ref[...]) + b1_ref[...]))
        h = jnp.sin(f2_ref[...] * (_dot3(h, w2_ref[...]) + b2_ref[...]))
        h_scr[...] = jnp.sin(f3_ref[...] * (_dot3(h, w3_ref[...]) + b3_ref[...]))

    t01 = z_ref[:, 0:1]
    decay = jnp.exp(-t01 * dl_ref[...])
    o_ref[...] = _dot3(h_scr[...], w4_ref[...]) * decay


@functools.lru_cache(maxsize=None)
def _filter_features(seq):
    _, n1, n2 = _fft_plan(seq)
    n1h = n1 // 2
    t01 = np.linspace(0.0, 1.0, seq, dtype=np.float32)[:, None]
    ang = (2.0 * math.pi * np.arange(seq, dtype=np.float32)[:, None] / seq).astype(np.float32)
    bands = np.linspace(1e-4, POS_BANDS - 1, POS_BANDS, dtype=np.float32)[None, :]
    arg = (bands * ang).astype(np.float32)
    z = np.concatenate([t01, np.cos(arg), -np.sin(arg)], axis=-1).astype(np.float32)
    pos = (np.arange(n1h)[None, :] * n2 + np.arange(n2)[:, None]).reshape(-1)
    zp = np.zeros((seq, FEAT_PAD), np.float32)
    zp[:, :z.shape[1]] = z[pos]
    return zp


def _filtergen(seq, w1, b1, f1, w2, b2, f2, w3, b3, f3, w4, hy_ch):
    fw = w1.shape[1]
    fo = w4.shape[1]
    z = jnp.asarray(_filter_features(seq))
    pe = z.shape[1]
    w1 = jnp.pad(w1, ((0, pe - w1.shape[0]), (0, 0)))
    deltas = np.abs(np.linspace(math.log(DECAY_FAST) / DECAY_TARGET, math.log(DECAY_SLOW) / DECAY_TARGET,
                                hy_ch, dtype=np.float32))
    dl = jnp.asarray(np.tile(deltas, fo // hy_ch).reshape(1, fo))
    tr = min(512, seq)
    tc = min(512, fo)
    row = lambda a: a.reshape(1, -1)
    full = lambda shp: pl.BlockSpec(shp, lambda i, j: (0, 0))
    return pl.pallas_call(
        _filtergen_kernel,
        grid=(seq // tr, fo // tc),
        in_specs=[
            pl.BlockSpec((tr, pe), lambda i, j: (i, 0)),
            full((pe, fw)), full((1, fw)), full((1, fw)),
            full((fw, fw)), full((1, fw)), full((1, fw)),
            full((fw, fw)), full((1, fw)), full((1, fw)),
            pl.BlockSpec((fw, tc), lambda i, j: (0, j)),
            pl.BlockSpec((1, tc), lambda i, j: (0, j)),
        ],
        out_specs=pl.BlockSpec((tr, tc), lambda i, j: (i, j)),
        out_shape=jax.ShapeDtypeStruct((seq, fo), F32),
        scratch_shapes=[pltpu.VMEM((tr, fw), F32)],
        compiler_params=_cparams(("arbitrary", "arbitrary")),
        name="filtergen",
    )(z, w1, row(b1), row(f1), w2, row(b2), row(f2), w3, row(b3), row(f3), w4, dl)


def _filterfft_kernel(hf_ref, hb_ref, f1_ref, g2_ref, o_ref, af_scr, ab_scr, *, n1, n2, k1c, scale):
    kc = pl.program_id(1)

    @pl.when(kc == 0)
    def _():
        _fft_stage1(lambda s: hf_ref[s].astype(BF16), f1_ref, af_scr, n1, n2)
        _fft_stage1(lambda s: hb_ref[s].astype(BF16), f1_ref, ab_scr, n1, n2)

    def body(kk, carry):
        k1 = kc * k1c + kk
        rows = jnp.concatenate([_load_freq_rows(af_scr, k1, n1, n2), _load_freq_rows(ab_scr, k1, n1, n2)], axis=1)
        x = jnp.dot(g2_ref[kk], rows, preferred_element_type=F32)
        xf, xb = x[:, :LANES], x[:, LANES:]
        o_ref[kk] = jnp.concatenate([xf[:n2] + xb[:n2], xf[n2:] - xb[n2:]], axis=0) * scale
        return carry
    lax.fori_loop(0, k1c, body, 0, unroll=min(k1c, STAGE2_UNROLL))


def _filter_spectrum(hs, seq, hy_ch):
    n, n1, n2 = _fft_plan(seq)
    n1h = n1 // 2
    cols = hs.shape[-1]
    cpt = hy_ch // LANES
    f1, _, g2, _ = _dft_constants(seq)
    f1r = jnp.asarray(f1[:, :n1h], BF16)
    g2b = jnp.asarray(g2, BF16)
    k1c = _k1_chunk(n1)
    hs3 = hs.reshape(n2, n1h, cols)
    scratch = pltpu.VMEM((n2 * _pitch(n1), LANES), jnp.uint32)
    return pl.pallas_call(
        functools.partial(_filterfft_kernel, n1=n1, n2=n2, k1c=k1c, scale=1.0 / n),
        grid=(2 * cpt, n1 // k1c),
        in_specs=[
            pl.BlockSpec((n2, n1h, LANES), lambda c, k: (0, 0, (c // cpt) * 2 * cpt + c % cpt)),
            pl.BlockSpec((n2, n1h, LANES), lambda c, k: (0, 0, (c // cpt) * 2 * cpt + cpt + c % cpt)),
            pl.BlockSpec((2 * n1, n1h), lambda c, k: (0, 0)),
            pl.BlockSpec((k1c, 2 * n2, 2 * n2), lambda c, k: (k, 0, 0)),
        ],
        out_specs=pl.BlockSpec((k1c, 2 * n2, LANES), lambda c, k: (k, 0, c)),
        out_shape=jax.ShapeDtypeStruct((n1, 2 * n2, 2 * hy_ch), F32),
        scratch_shapes=[scratch, scratch],
        compiler_params=_cparams(("arbitrary", "arbitrary")),
        name="filterfft",
    )(hs3, hs3, f1r, g2b)


def _k1_chunk(n1):
    return n1 if n1 <= 64 else 8


def _hyena_kernel(u_ref, g_ref, skip_ref, h_ref, f1_ref, f1i_ref, g2_ref, g2i_ref, o_ref, a_scr,
                  *z_scr, n1, n2, k1c):
    kc = pl.program_id(2)
    n1h = n1 // 2
    zp = _pitch(n1h)

    @pl.when(kc == 0)
    def _():
        _fft_stage1(lambda s: jnp.concatenate([u_ref[0, s], u_ref[1, s]], axis=0), f1_ref, a_scr, n1, n2)

    def body(kk, carry):
        k1 = kc * k1c + kk
        x = jnp.dot(g2_ref[kk], _load_freq_rows(a_scr, k1, n1, n2), preferred_element_type=F32)
        xr, xi = x[:n2], x[n2:]
        h = h_ref[kk]
        hr, hi = h[:n2], h[n2:]
        y = jnp.concatenate([xr * hr - xi * hi, xr * hi + xi * hr], axis=0).astype(BF16)
        bq = jnp.dot(g2i_ref[kk], y, preferred_element_type=F32)
        a_scr[pl.ds(k1, n2, stride=_pitch(n1)), :] = _pack_complex(bq[:n2], bq[n2:])
        return carry
    lax.fori_loop(0, k1c, body, 0, unroll=min(k1c, STAGE2_UNROLL))

    @pl.when(kc == pl.num_programs(2) - 1)
    def _():
        skip = skip_ref[...]

        def body3(s, carry):
            re, im = _unpack_complex(a_scr[pl.ds(pl.multiple_of(s * _pitch(n1), SUBLANES), n1), :])
            y = jnp.dot(f1i_ref[...], jnp.concatenate([re, im], axis=0), preferred_element_type=F32)
            for bb in range(2):
                yb = y[bb * n1h:(bb + 1) * n1h]
                z = g_ref[bb, s].astype(F32) * (yb + skip * u_ref[bb, s].astype(F32))
                if z_scr:
                    z_scr[0][bb, pl.ds(pl.multiple_of(s * zp, SUBLANES), n1h), :] = z
                else:
                    o_ref[bb, s] = z.astype(o_ref.dtype)
            return carry
        lax.fori_loop(0, n2, body3, 0, unroll=FFT_UNROLL)

        if z_scr:
            def body4(r, carry):
                for bb in range(2):
                    rows = z_scr[0][bb, pl.ds(r, n2, stride=zp), :]
                    o_ref[bb, pl.ds(pl.multiple_of(r * n2, n2), n2), :] = rows.astype(o_ref.dtype)
                return carry
            lax.fori_loop(0, n1h, body4, 0, unroll=FFT_UNROLL)


def _hyena_conv(u_arr, u_col, g_arr, g_col, skip, tf, order, seq, hy_ch, token_order):
    b = u_arr.shape[0]
    n, n1, n2 = _fft_plan(seq)
    n1h = n1 // 2
    f1, f1i, g2, g2i = _dft_constants(seq)
    k1c = _k1_chunk(n1)
    ucb, gcb = u_col // LANES, g_col // LANES
    hcb = order * hy_ch // LANES
    scratch = [pltpu.VMEM((n2 * _pitch(n1), LANES), jnp.uint32)]
    if token_order:
        out_spec = pl.BlockSpec((2, seq, LANES), lambda c, p, k: (p, 0, c))
        out_shape = jax.ShapeDtypeStruct((b, seq, hy_ch), BF16)
        scratch.append(pltpu.VMEM((2, n2 * _pitch(n1h), LANES), F32))
    else:
        out_spec = pl.BlockSpec((2, n2, n1h, LANES), lambda c, p, k: (p, 0, 0, c))
        out_shape = jax.ShapeDtypeStruct((b, n2, n1h, hy_ch), BF16)
    return pl.pallas_call(
        functools.partial(_hyena_kernel, n1=n1, n2=n2, k1c=k1c),
        grid=(hy_ch // LANES, b // 2, n1 // k1c),
        in_specs=[
            pl.BlockSpec((2, n2, n1h, LANES), lambda c, p, k: (p, 0, 0, ucb + c)),
            pl.BlockSpec((2, n2, n1h, LANES), lambda c, p, k: (p, 0, 0, gcb + c)),
            pl.BlockSpec((1, LANES), lambda c, p, k: (0, c)),
            pl.BlockSpec((k1c, 2 * n2, LANES), lambda c, p, k: (k, 0, hcb + c)),
            pl.BlockSpec((2 * n1, n1), lambda c, p, k: (0, 0)),
            pl.BlockSpec((n1, 2 * n1), lambda c, p, k: (0, 0)),
            pl.BlockSpec((k1c, 2 * n2, 2 * n2), lambda c, p, k: (k, 0, 0)),
            pl.BlockSpec((k1c, 2 * n2, 2 * n2), lambda c, p, k: (k, 0, 0)),
        ],
        out_specs=out_spec,
        out_shape=out_shape,
        scratch_shapes=scratch,
        compiler_params=_cparams(("arbitrary", "arbitrary", "arbitrary")),
        name="hyenaconv",
    )(u_arr, g_arr, skip.reshape(1, hy_ch), tf,
      jnp.asarray(f1, BF16), jnp.asarray(f1i, BF16), jnp.asarray(g2, BF16), jnp.asarray(g2i, BF16))


def _outproj_kernel(a_ref, z_ref, x_ref, wa_ref, wh_ref, gh_ref, gf_ref, wrh_ref, wrl_ref, br_ref, c0_ref,
                    tri_ref, x1_ref, xn_ref, idx_ref, gate_ref, rank_ref, cnt_ref, run_scr, *, n_exp):
    i = pl.program_id(0)

    @pl.when(i == 0)
    def _():
        run_scr[...] = c0_ref[...]

    z = z_ref[...].astype(F32)
    zn = (z * lax.rsqrt(jnp.mean(z * z, axis=-1, keepdims=True) + EPS) * gh_ref[...]).astype(BF16)
    mixed = (jnp.dot(a_ref[...], wa_ref[...], preferred_element_type=F32)
             + jnp.dot(zn, wh_ref[...], preferred_element_type=F32))
    x1 = x_ref[...] + mixed
    x1_ref[...] = x1
    xn = x1 * lax.rsqrt(jnp.mean(x1 * x1, axis=-1, keepdims=True) + EPS) * gf_ref[...]
    half = xn.shape[1] // 2
    packed = _pack_complex(xn[:, :half], xn[:, half:])
    for c in range(SUBLANES):
        xn_ref[pl.ds(c, xn.shape[0], stride=SUBLANES), :] = packed[:, c * LANES:(c + 1) * LANES]

    xh, xl = _split_bf16(xn)
    dg = functools.partial(lax.dot_general, dimension_numbers=NT_DIMS, preferred_element_type=F32)
    logit = dg(wrh_ref[...], xh) + dg(wrh_ref[...], xl) + dg(wrl_ref[...], xh) + br_ref[...]
    tm = logit.shape[1]
    eio = lax.broadcasted_iota(jnp.int32, (n_exp, tm), 0)
    vals, onehots = [], []
    cur = logit
    for k in range(TOP_K):
        m = jnp.max(cur, axis=0, keepdims=True)
        idx = jnp.min(jnp.where(cur == m, eio, n_exp), axis=0, keepdims=True)
        oh = eio == idx
        vals.append(m)
        onehots.append(oh)
        idx_ref[k:k + 1, :] = idx
        cur = jnp.where(oh, -jnp.inf, cur)
    ex = [jnp.exp(v - vals[0]) for v in vals]
    tot = ex[0] + ex[1] + ex[2] + ex[3]
    for k in range(TOP_K):
        gate_ref[k:k + 1, :] = ex[k] / tot
    ohsum = jnp.zeros((n_exp, tm), F32)
    for oh in onehots:
        ohsum = ohsum + oh.astype(F32)
    base = jnp.dot(ohsum.astype(BF16), tri_ref[...], preferred_element_type=F32) + run_scr[...]
    for k in range(TOP_K):
        rank_ref[k:k + 1, :] = jnp.sum(jnp.where(onehots[k], base, 0.0), axis=0, keepdims=True).astype(jnp.int32)
    run_scr[...] = run_scr[...] + jnp.sum(ohsum, axis=1, keepdims=True)
    cnt_ref[...] = run_scr[...]


def _outproj_router(attn_n, z2, x2d, wa, wh, g_hy, g_ffn, wr_hi, wr_lo, b_router, count0):
    t, d = x2d.shape
    aw, hc = attn_n.shape[1], z2.shape[1]
    n_exp = wr_hi.shape[0]
    tm = min(512, t)
    tri = jnp.asarray(np.triu(np.ones((tm, tm), np.float32), k=1), BF16)
    full = lambda shp: pl.BlockSpec(shp, lambda i: (0, 0))
    resident = lambda shp: pl.BlockSpec(shp, lambda i: (0, 0), pipeline_mode=pl.Buffered(1))
    rowblk = lambda w: pl.BlockSpec((tm, w), lambda i: (i, 0))
    colblk = pl.BlockSpec((TOP_K, tm), lambda i: (0, i))
    return pl.pallas_call(
        functools.partial(_outproj_kernel, n_exp=n_exp),
        grid=(t // tm,),
        in_specs=[rowblk(aw), rowblk(hc), rowblk(d), resident((aw, d)), resident((hc, d)), full((1, hc)),
                  full((1, d)), full((n_exp, d)), full((n_exp, d)), full((n_exp, 1)), full((n_exp, 1)),
                  resident((tm, tm))],
        out_specs=[rowblk(d), pl.BlockSpec((tm * SUBLANES, LANES), lambda i: (i, 0)),
                   colblk, colblk, colblk, full((n_exp, 1))],
        out_shape=[jax.ShapeDtypeStruct((t, d), F32), jax.ShapeDtypeStruct((t * SUBLANES, LANES), jnp.uint32),
                   jax.ShapeDtypeStruct((TOP_K, t), jnp.int32), jax.ShapeDtypeStruct((TOP_K, t), F32),
                   jax.ShapeDtypeStruct((TOP_K, t), jnp.int32), jax.ShapeDtypeStruct((n_exp, 1), F32)],
        scratch_shapes=[pltpu.VMEM((n_exp, 1), F32)],
        compiler_params=_cparams(("arbitrary",)),
        name="outproj_router",
    )(attn_n, z2, x2d, wa, wh, g_hy.reshape(1, hc), g_ffn.reshape(1, d), wr_hi, wr_lo,
      b_router.reshape(n_exp, 1), count0, tri)


def _dest_kernel(ps_ref, idx_ref, rank_ref, o_ref, *, n_exp):
    idx = idx_ref[...]
    acc = rank_ref[...]
    for e in range(n_exp):
        acc = acc + jnp.where(idx == e, ps_ref[e], 0)
    o_ref[...] = acc


def _dest_rows(pstart, idx, rank):
    k, t = idx.shape
    tb = next(c for c in (2048, 1024, 512, 256, 128) if t % c == 0)
    blk = pl.BlockSpec((k, tb), lambda i: (0, i))
    return pl.pallas_call(
        functools.partial(_dest_kernel, n_exp=pstart.shape[0]),
        grid=(t // tb,),
        in_specs=[pl.BlockSpec(memory_space=pltpu.SMEM), blk, blk],
        out_specs=blk,
        out_shape=jax.ShapeDtypeStruct((k, t), jnp.int32),
        compiler_params=_cparams(("arbitrary",)),
        name="dest",
    )(pstart, idx, rank)


def _zero_unrouted_rows(ps_ref, cnt_ref, xs_ref, zero_scr, sem):
    zero_scr[...] = jnp.zeros(zero_scr.shape, zero_scr.dtype)
    n_exp = ps_ref.shape[0]

    def fill(first, count):
        def issue(r, c):
            pltpu.make_async_copy(zero_scr, xs_ref.at[first + r], sem).start()
            return c
        lax.fori_loop(0, count, issue, 0)

        def drain(r, c):
            pltpu.make_async_copy(zero_scr, xs_ref.at[0], sem).wait()
            return c
        lax.fori_loop(0, count, drain, 0)

    def pad_rows(e):
        return lax.rem(MOE_ROWS - lax.rem(cnt_ref[e], MOE_ROWS), MOE_ROWS)

    def per_expert(e, carry):
        fill(ps_ref[e] + cnt_ref[e], pad_rows(e))
        return carry
    lax.fori_loop(0, n_exp, per_expert, 0)
    end = ps_ref[n_exp - 1] + cnt_ref[n_exp - 1] + pad_rows(n_exp - 1)
    fill(end, xs_ref.shape[0] - end)


def _dispatch_kernel(ps_ref, cnt_ref, dest_ref, xa_ref, xb_ref, xs_ref, sem, zero_scr, *, tiles_a):
    i = pl.program_id(0)
    tm = xa_ref.shape[0] // SUBLANES

    @pl.when(i == 0)
    def _():
        _zero_unrouted_rows(ps_ref, cnt_ref, xs_ref, zero_scr, sem)

    def scatter_tile(xn_ref):
        def row_copy(t, d):
            src = xn_ref.at[pl.ds(pl.multiple_of(t * SUBLANES, SUBLANES), SUBLANES)]
            return pltpu.make_async_copy(src, xs_ref.at[d], sem)

        def issue(t, carry):
            for k in range(TOP_K):
                row_copy(t, dest_ref[k, t]).start(priority=k % 2)
            return carry
        lax.fori_loop(0, tm, issue, 0)

        def drain(t, carry):
            for k in range(TOP_K):
                row_copy(0, 0).wait()
            return carry
        lax.fori_loop(0, tm, drain, 0)

    @pl.when(i < tiles_a)
    def _():
        scatter_tile(xa_ref)

    @pl.when(i >= tiles_a)
    def _():
        scatter_tile(xb_ref)


def _dispatch(pstart, cnt, dest, xn_a, xn_b, n_rows):
    t = dest.shape[1]
    ta = xn_a.shape[0] // SUBLANES
    tm = min(256, ta, t - ta)
    assert ta % tm == 0 and (t - ta) % tm == 0
    tiles_a = ta // tm
    return pl.pallas_call(
        functools.partial(_dispatch_kernel, tiles_a=tiles_a),
        grid=(t // tm,),
        in_specs=[pl.BlockSpec(memory_space=pltpu.SMEM), pl.BlockSpec(memory_space=pltpu.SMEM),
                  pl.BlockSpec((TOP_K, tm), lambda i: (0, i), memory_space=pltpu.SMEM),
                  pl.BlockSpec((tm * SUBLANES, LANES), lambda i: (jnp.minimum(i, tiles_a - 1), 0)),
                  pl.BlockSpec((tm * SUBLANES, LANES), lambda i: (jnp.maximum(i - tiles_a, 0), 0))],
        out_specs=pl.BlockSpec(memory_space=pl.ANY),
        out_shape=jax.ShapeDtypeStruct((n_rows, SUBLANES, LANES), jnp.uint32),
        scratch_shapes=[pltpu.SemaphoreType.DMA(()), pltpu.VMEM((SUBLANES, LANES), jnp.uint32)],
        compiler_params=_cparams(("arbitrary",)),
        name="dispatch",
    )(pstart, cnt, dest, xn_a, xn_b)


def _expert_kernel(be_ref, nu_ref, x_ref, wgu_ref, bgu_ref, wd_ref, bd_ref, y_ref, x_scr, *, tf):
    del be_ref
    bm, d = x_scr.shape
    half = d // 2
    f = wd_ref.shape[1]

    @pl.when(pl.program_id(0) < nu_ref[0])
    def _():
        for c in range(SUBLANES):
            hi, lo = _unpack_complex(x_ref[pl.ds(c, bm, stride=SUBLANES), :])
            x_scr[:, c * LANES:(c + 1) * LANES] = hi
            x_scr[:, half + c * LANES:half + (c + 1) * LANES] = lo
        x = x_scr[...]
        acc = bd_ref[0]
        for c0 in range(0, f, tf):
            g = jnp.dot(x, wgu_ref[0, :, c0:c0 + tf], preferred_element_type=F32) + bgu_ref[0, :, c0:c0 + tf]
            u = (jnp.dot(x, wgu_ref[0, :, f + c0:f + c0 + tf], preferred_element_type=F32)
                 + bgu_ref[0, :, f + c0:f + c0 + tf])
            glu = jnp.minimum(g, SWIGLU_LIMIT)
            lin = jnp.clip(u, -SWIGLU_LIMIT, SWIGLU_LIMIT)
            h = glu * jax.nn.sigmoid(SWIGLU_ALPHA * glu) * (lin + 1.0)
            acc = acc + jnp.dot(h.astype(BF16), wd_ref[0, c0:c0 + tf, :], preferred_element_type=F32)
        for c in range(SUBLANES):
            y_ref[pl.ds(c, bm, stride=SUBLANES), :] = _pack_complex(
                acc[:, c * LANES:(c + 1) * LANES], acc[:, half + c * LANES:half + (c + 1) * LANES])


def _experts(block_expert, n_used, xs, w_gu, b_gu, w_dn, b_dn):
    n_exp, d, f2 = w_gu.shape
    n_rows = xs.shape[0] // SUBLANES
    f = f2 // 2
    tf = next(c for c in (512, 256, 128) if f % c == 0)
    bm = MOE_ROWS
    last = lambda i, nu: jnp.minimum(i, nu[0] - 1)
    resident = pl.Buffered(1)
    grid_spec = pltpu.PrefetchScalarGridSpec(
        num_scalar_prefetch=2,
        grid=(n_rows // bm,),
        in_specs=[
            pl.BlockSpec((bm * SUBLANES, LANES), lambda i, be, nu: (last(i, nu), 0)),
            pl.BlockSpec((1, d, f2), lambda i, be, nu: (be[i], 0, 0), pipeline_mode=resident),
            pl.BlockSpec((1, 1, f2), lambda i, be, nu: (be[i], 0, 0)),
            pl.BlockSpec((1, f, d), lambda i, be, nu: (be[i], 0, 0), pipeline_mode=resident),
            pl.BlockSpec((1, 1, d), lambda i, be, nu: (be[i], 0, 0)),
        ],
        out_specs=pl.BlockSpec((bm * SUBLANES, LANES), lambda i, be, nu: (last(i, nu), 0)),
        scratch_shapes=[pltpu.VMEM((bm, d), BF16)],
    )
    return pl.pallas_call(
        functools.partial(_expert_kernel, tf=tf),
        grid_spec=grid_spec,
        out_shape=jax.ShapeDtypeStruct((n_rows * SUBLANES, LANES), jnp.uint32),
        compiler_params=_cparams(("arbitrary",)),
        name="experts",
    )(block_expert, n_used, xs, w_gu, b_gu.reshape(n_exp, 1, f2), w_dn, b_dn.reshape(n_exp, 1, d))


def _combine_kernel(dest_ref, dest_next_ref, x1_ref, gate_ref, ys_ref, o_ref, buf, sems):
    i = pl.program_id(0)
    tm, d = x1_ref.shape
    half = d // 2
    slot = lax.rem(i, 2)

    def row_copy(s, t, k, d_row):
        dst = buf.at[s, k, pl.ds(pl.multiple_of(t * SUBLANES, SUBLANES), SUBLANES)]
        return pltpu.make_async_copy(ys_ref.at[d_row], dst, sems.at[s])

    def issue_tile(d_ref, s):
        def issue(t, carry):
            for k in range(TOP_K):
                row_copy(s, t, k, d_ref[k, t]).start(priority=k % 2)
            return carry
        lax.fori_loop(0, tm, issue, 0)

    @pl.when(i == 0)
    def _():
        issue_tile(dest_ref, 0)

    @pl.when(i + 1 < pl.num_programs(0))
    def _():
        issue_tile(dest_next_ref, 1 - slot)

    def drain(t, carry):
        for k in range(TOP_K):
            row_copy(slot, 0, 0, 0).wait()
        return carry
    lax.fori_loop(0, tm, drain, 0)

    gate = gate_ref[...]
    for c in range(SUBLANES):
        lo_cols = slice(c * LANES, (c + 1) * LANES)
        hi_cols = slice(half + c * LANES, half + (c + 1) * LANES)
        moe_a = x1_ref[:, lo_cols]
        moe_b = x1_ref[:, hi_cols]
        for k in range(TOP_K):
            w = buf[slot, k, pl.ds(c, tm, stride=SUBLANES), :]
            g = gate[:, k:k + 1]
            moe_a = moe_a + g * lax.bitcast_convert_type(w & jnp.uint32(0xFFFF0000), F32)
            moe_b = moe_b + g * lax.bitcast_convert_type(w << 16, F32)
        o_ref[:, lo_cols] = moe_a
        o_ref[:, hi_cols] = moe_b


def _combine(dest, x1, gate_tk, ys):
    t, d = x1.shape
    tm = min(256, t)
    nt = t // tm
    return pl.pallas_call(
        _combine_kernel,
        grid=(nt,),
        in_specs=[pl.BlockSpec((TOP_K, tm), lambda i: (0, i), memory_space=pltpu.SMEM),
                  pl.BlockSpec((TOP_K, tm), lambda i: (0, jnp.minimum(i + 1, nt - 1)), memory_space=pltpu.SMEM),
                  pl.BlockSpec((tm, d), lambda i: (i, 0)),
                  pl.BlockSpec((tm, TOP_K), lambda i: (i, 0)),
                  pl.BlockSpec(memory_space=pl.ANY)],
        out_specs=pl.BlockSpec((tm, d), lambda i: (i, 0)),
        out_shape=jax.ShapeDtypeStruct((t, d), F32),
        scratch_shapes=[pltpu.VMEM((2, TOP_K, tm * SUBLANES, LANES), jnp.uint32), pltpu.SemaphoreType.DMA((2,))],
        compiler_params=_cparams(("arbitrary",)),
        name="combine",
    )(dest, dest, x1, gate_tk, ys)


def _mixer(x3, g_mix, w_in_bf, g_q, g_k, sinks, w_short, b_short, flt, hyena_skip, g_attn_out, dims):
    n_heads, kv_w, hy_ch = dims
    b, seq, d = x3.shape
    attn_w = n_heads * HEAD_DIM
    proj = _inproj(x3.reshape(b * seq, d), g_mix, w_in_bf, g_q, g_k, attn_w, kv_w)
    proj3 = proj.reshape(b, seq, proj.shape[1])
    attn_n = _attention(proj3, sinks, g_attn_out, n_heads, kv_w)
    u = _shortconv(proj3, w_short, b_short, attn_w + 2 * kv_w)
    hs = _filtergen(seq, *flt, hy_ch)
    tf = _filter_spectrum(hs, seq, hy_ch)
    z1 = _hyena_conv(u, 0, u, hy_ch, hyena_skip[0], tf, 0, seq, hy_ch, token_order=False)
    z2 = _hyena_conv(z1, 0, u, 2 * hy_ch, hyena_skip[1], tf, 1, seq, hy_ch, token_order=True)
    return attn_n.reshape(b * seq, attn_w), z2.reshape(b * seq, hy_ch)


def kernel(x_prompt, x_sample, g_mix, w_in, g_q, g_k, attn_sinks, w_short, b_short, flt_w1, flt_b1, flt_freq1, flt_w2, flt_b2, flt_freq2, flt_w3, flt_b3, flt_freq3, flt_w4, hyena_skip, g_attn_out, g_hyena_out, w_out, g_ffn, w_router, b_router, w_gate_up, b_gate_up, w_down, b_down):
    assert g_mix.shape[0] == 1, "single-layer trunk"
    d = x_prompt.shape[-1]
    assert d == 2 * SUBLANES * LANES, "routed-row tile layout is built for d_model = 2048"
    n_heads = attn_sinks.shape[-1]
    hy_ch = hyena_skip.shape[-1]
    attn_w = n_heads * HEAD_DIM
    kv_w = (w_in.shape[-1] - attn_w - 3 * hy_ch) // 2
    n_exp = w_router.shape[-1]
    dims = (n_heads, kv_w, hy_ch)

    w_in_bf = w_in[0].astype(BF16)
    wa = w_out[0, :attn_w].astype(BF16)
    wh = w_out[0, attn_w:].astype(BF16)
    wr_t = w_router[0].T
    wr_hi = wr_t.astype(BF16)
    wr_lo = (wr_t - wr_hi.astype(F32)).astype(BF16)
    w_gu = w_gate_up[0].astype(BF16)
    w_dn = w_down[0].astype(BF16)
    flt = (flt_w1[0], flt_b1[0], flt_freq1[0], flt_w2[0], flt_b2[0], flt_freq2[0],
           flt_w3[0], flt_b3[0], flt_freq3[0], flt_w4[0])

    xs_in = (x_prompt, x_sample)
    routed = []
    counts = jnp.zeros((n_exp, 1), F32)
    for x3 in xs_in:
        attn_n, z2 = _mixer(x3, g_mix[0], w_in_bf, g_q[0], g_k[0], attn_sinks[0], w_short[0], b_short[0],
                            flt, hyena_skip[0], g_attn_out[0], dims)
        x1, xn, idx, gate, rank, counts = _outproj_router(
            attn_n, z2, x3.reshape(-1, d), wa, wh, g_hyena_out[0], g_ffn[0], wr_hi, wr_lo, b_router[0], counts)
        routed.append((x1, xn, idx, gate, rank))

    n_slots = sum(r[0].shape[0] for r in routed) * TOP_K
    n_rows = (n_slots // MOE_ROWS + n_exp) * MOE_ROWS
    cnt = counts[:, 0].astype(jnp.int32)
    padded = (cnt + MOE_ROWS - 1) // MOE_ROWS * MOE_ROWS
    pend = jnp.cumsum(padded)
    pstart = (pend - padded).astype(jnp.int32)
    n_blocks = n_rows // MOE_ROWS
    block_row0 = jnp.arange(n_blocks, dtype=jnp.int32) * MOE_ROWS
    block_expert = jnp.minimum(jnp.sum(pend[None, :] <= block_row0[:, None], axis=1), n_exp - 1).astype(jnp.int32)
    n_used = (pend[-1:] // MOE_ROWS).astype(jnp.int32)

    dest_all = _dest_rows(pstart, jnp.concatenate([r[2] for r in routed], axis=1),
                          jnp.concatenate([r[4] for r in routed], axis=1))
    t_a = routed[0][0].shape[0]
    dests = [dest_all[:, :t_a], dest_all[:, t_a:]]
    xs = _dispatch(pstart, cnt, dest_all, routed[0][1], routed[1][1], n_rows)
    ys = _experts(block_expert, n_used, xs.reshape(n_rows * SUBLANES, LANES), w_gu, b_gate_up[0], w_dn, b_down[0])
    ys = ys.reshape(n_rows, SUBLANES, LANES)
    outs = []
    for (x1, xn, idx, gate, rank), dest, x3 in zip(routed, dests, xs_in):
        outs.append(_combine(dest, x1, gate.T, ys).reshape(x3.shape))
    return tuple(outs)
```

```python
import functools
import math

import numpy as np
import jax
import jax.numpy as jnp
from jax import lax
from jax.experimental import pallas as pl
from jax.experimental.pallas import tpu as pltpu

HEAD_DIM = 64
WINDOW = 128
TOP_K = 4
EPS = 1e-6
POS_BANDS = 16
FEAT_PAD = 64
DECAY_FAST = 0.3
DECAY_SLOW = 1.5
DECAY_TARGET = 1e-2
SWIGLU_ALPHA = 1.702
SWIGLU_LIMIT = 7.0

LANES = 128
SUBLANES = 8
VMEM_LIMIT = 56 * 1024 * 1024
MOE_ROWS = 512
ZERO_ROWS = 64
FFT_UNROLL = 8
STAGE2_UNROLL = 8
F32 = jnp.float32
BF16 = jnp.bfloat16
NT_DIMS = (((1,), (1,)), ((), ()))


def _cparams(sem):
    return pltpu.CompilerParams(dimension_semantics=sem, vmem_limit_bytes=VMEM_LIMIT)


def _split_bf16(a):
    hi = a.astype(BF16)
    lo = (a - hi.astype(F32)).astype(BF16)
    return hi, lo


def _dot3(a, b):
    ah, al = _split_bf16(a)
    bh, bl = _split_bf16(b)
    d = functools.partial(jnp.dot, preferred_element_type=F32)
    return d(ah, bh) + d(ah, bl) + d(al, bh)


def _inproj_kernel(x_ref, g_ref, w_ref, gq_ref, gk_ref, s_ref, o_ref, *, n_q_tiles, tn):
    x = x_ref[...]
    ms = jnp.mean(x * x, axis=-1, keepdims=True)
    h = (x * lax.rsqrt(ms + EPS) * g_ref[...]).astype(BF16)

    def head_norm(a, g):
        hi, lo = _split_bf16(a * a)
        s = s_ref[:a.shape[1], :a.shape[1]]
        ss = (jnp.dot(hi, s, preferred_element_type=F32)
              + jnp.dot(lo, s, preferred_element_type=F32))
        return a * lax.rsqrt(ss * (1.0 / HEAD_DIM) + EPS) * g

    for j in range(w_ref.shape[1] // tn):
        cols = slice(j * tn, (j + 1) * tn)
        acc = jnp.dot(h, w_ref[:, cols], preferred_element_type=F32)
        if j < n_q_tiles:
            o_ref[:, cols] = head_norm(acc, gq_ref[...]).astype(o_ref.dtype)
        elif j == n_q_tiles:
            half = tn // 2
            o_ref[:, j * tn:j * tn + half] = head_norm(acc[:, :half], gk_ref[...]).astype(o_ref.dtype)
            o_ref[:, j * tn + half:(j + 1) * tn] = acc[:, half:].astype(o_ref.dtype)
        else:
            o_ref[:, cols] = acc.astype(o_ref.dtype)


def _inproj(x2d, g_mix, w_in_bf, g_q, g_k, attn_w, kv_w):
    t, d = x2d.shape
    in_w = w_in_bf.shape[1]
    tn = 2 * kv_w
    tm = min(512, t)
    assert attn_w % tn == 0 and in_w % tn == 0 and t % tm == 0
    n_q_tiles = attn_w // tn
    heads_per_tile = tn // HEAD_DIM
    blk = np.kron(np.eye(heads_per_tile, dtype=np.float32), np.ones((HEAD_DIM, HEAD_DIM), np.float32))
    gq_t = jnp.tile(g_q.reshape(1, HEAD_DIM), (1, heads_per_tile))
    gk_t = jnp.tile(g_k.reshape(1, HEAD_DIM), (1, heads_per_tile // 2))
    return pl.pallas_call(
        functools.partial(_inproj_kernel, n_q_tiles=n_q_tiles, tn=tn),
        grid=(t // tm,),
        in_specs=[
            pl.BlockSpec((tm, d), lambda i: (i, 0)),
            pl.BlockSpec((1, d), lambda i: (0, 0)),
            pl.BlockSpec((d, in_w), lambda i: (0, 0), pipeline_mode=pl.Buffered(1)),
            pl.BlockSpec((1, tn), lambda i: (0, 0)),
            pl.BlockSpec((1, tn // 2), lambda i: (0, 0)),
            pl.BlockSpec((tn, tn), lambda i: (0, 0)),
        ],
        out_specs=pl.BlockSpec((tm, in_w), lambda i: (i, 0)),
        out_shape=jax.ShapeDtypeStruct((t, in_w), BF16),
        compiler_params=_cparams(("arbitrary",)),
        name="inproj",
    )(x2d, g_mix.reshape(1, d), w_in_bf, gq_t, gk_t, jnp.asarray(blk, BF16))


def _kv_start(i, tq, span, seq):
    return jnp.clip(i * tq - WINDOW, 0, seq - span)


def _attn_kernel(q_ref, k_ref, v_ref, bias_ref, sink_ref, g_ref, o_ref, o_scr, s_scr, p_scr,
                 *, n_heads, n_kv, tq, span, seq):
    i = pl.program_id(1)
    grp = n_heads // n_kv
    rows = 2 * SUBLANES
    start = pl.multiple_of(_kv_start(i, tq, span, seq), WINDOW)
    for kh in range(n_kv):
        kk = k_ref[0, pl.ds(start, span), kh * HEAD_DIM:(kh + 1) * HEAD_DIM] * (HEAD_DIM ** -0.5)
        for h in range(kh * grp, (kh + 1) * grp):
            q = q_ref[0, :, h * HEAD_DIM:(h + 1) * HEAD_DIM]
            s_scr[h * tq:(h + 1) * tq, :] = (lax.dot_general(q, kk, NT_DIMS, preferred_element_type=F32)
                                             + bias_ref[0, h * tq:(h + 1) * tq, :])
    for h in range(n_heads):
        sink = sink_ref[h]
        for r in range(h * tq, (h + 1) * tq, rows):
            s = s_scr[r:r + rows, :]
            m = jnp.maximum(jnp.max(s, axis=-1, keepdims=True), sink)
            p = jnp.exp(s - m)
            denom = jnp.sum(p, axis=-1, keepdims=True) + jnp.exp(sink - m)
            p_scr[r:r + rows, :] = (p / denom).astype(BF16)
    for kh in range(n_kv):
        vv = v_ref[0, pl.ds(start, span), kh * HEAD_DIM:(kh + 1) * HEAD_DIM]
        for h in range(kh * grp, (kh + 1) * grp):
            o_scr[:, h * HEAD_DIM:(h + 1) * HEAD_DIM] = jnp.dot(p_scr[h * tq:(h + 1) * tq, :], vv,
                                                                preferred_element_type=F32)
    a = o_scr[...]
    ms = jnp.mean(a * a, axis=-1, keepdims=True)
    o_ref[0] = (a * lax.rsqrt(ms + EPS) * g_ref[...]).astype(o_ref.dtype)


@functools.lru_cache(maxsize=None)
def _alibi_window_bias(n_heads, tq, span):
    slopes = np.exp2(-8.0 * np.arange(1, n_heads + 1, dtype=np.float32) / n_heads).astype(np.float32)
    r = np.arange(tq)[:, None]
    c = np.arange(span)[None, :]
    tables = []
    for v in range(3):
        dist = np.abs(r - c + v * WINDOW)
        bias = np.where(dist <= WINDOW, -slopes[:, None, None] * dist.astype(np.float32), -np.inf)
        tables.append(bias.reshape(n_heads * tq, span))
    return np.stack(tables).astype(np.float32)


def _attention(proj3, sinks, g_attn_out, n_heads, kv_w):
    b, seq, _ = proj3.shape
    attn_w = n_heads * HEAD_DIM
    n_kv = kv_w // HEAD_DIM
    tq = WINDOW
    span = min(3 * WINDOW, seq)
    kcol = attn_w // kv_w
    bias = jnp.asarray(_alibi_window_bias(n_heads, tq, span))
    return pl.pallas_call(
        functools.partial(_attn_kernel, n_heads=n_heads, n_kv=n_kv, tq=tq, span=span, seq=seq),
        grid=(b, seq // tq),
        in_specs=[
            pl.BlockSpec((1, tq, attn_w), lambda bi, i: (bi, i, 0)),
            pl.BlockSpec((1, seq, kv_w), lambda bi, i: (bi, 0, kcol)),
            pl.BlockSpec((1, seq, kv_w), lambda bi, i: (bi, 0, kcol + 1)),
            pl.BlockSpec((1, n_heads * tq, span),
                         lambda bi, i: ((i * tq - _kv_start(i, tq, span, seq)) // WINDOW, 0, 0)),
            pl.BlockSpec(memory_space=pltpu.SMEM),
            pl.BlockSpec((1, attn_w), lambda bi, i: (0, 0)),
        ],
        out_specs=pl.BlockSpec((1, tq, attn_w), lambda bi, i: (bi, i, 0)),
        out_shape=jax.ShapeDtypeStruct((b, seq, attn_w), BF16),
        scratch_shapes=[pltpu.VMEM((tq, attn_w), F32), pltpu.VMEM((n_heads * tq, span), F32),
                        pltpu.VMEM((n_heads * tq, span), BF16)],
        compiler_params=_cparams(("arbitrary", "arbitrary")),
        name="attention",
    )(proj3, proj3, proj3, bias, sinks.astype(F32), g_attn_out.reshape(1, attn_w))


def _fft_plan(seq):
    n = 2 * seq
    n2 = 1 << (int(math.log2(n)) // 2)
    n1 = n // n2
    return n, n1, n2


@functools.lru_cache(maxsize=None)
def _dft_constants(seq):
    n, n1, n2 = _fft_plan(seq)
    n1h = n1 // 2

    def blockc(re, im):
        return np.block([[re, -im], [im, re]])

    k1 = np.arange(n1)[:, None]
    a1 = 2.0 * np.pi * (k1 * np.arange(n1h)[None, :]) / n1
    f1 = blockc(np.cos(a1), -np.sin(a1))
    f1i = blockc(np.cos(a1.T), np.sin(a1.T))
    kk1 = np.arange(n1)[:, None, None]
    kk2 = np.arange(n2)[None, :, None]
    nn2 = np.arange(n2)[None, None, :]
    a2 = 2.0 * np.pi * (nn2 * (kk1 + n1 * kk2)) / n
    c2, s2 = np.cos(a2), np.sin(a2)
    g2 = np.concatenate([np.concatenate([c2, s2], axis=2),
                         np.concatenate([-s2, c2], axis=2)], axis=1)
    c2t, s2t = np.swapaxes(c2, 1, 2), np.swapaxes(s2, 1, 2)
    g2i = np.concatenate([np.concatenate([c2t, -s2t], axis=2),
                          np.concatenate([s2t, c2t], axis=2)], axis=1)
    return (f1.astype(np.float32), f1i.astype(np.float32),
            g2.astype(np.float32), g2i.astype(np.float32))


def _pack_complex(re, im):
    rb = lax.bitcast_convert_type(re.astype(BF16).astype(F32), jnp.uint32)
    ib = lax.bitcast_convert_type(im.astype(BF16).astype(F32), jnp.uint32)
    return (rb & jnp.uint32(0xFFFF0000)) | (ib >> 16)


def _unpack_complex(w):
    re = lax.bitcast_convert_type(w & jnp.uint32(0xFFFF0000), F32).astype(BF16)
    im = lax.bitcast_convert_type(w << 16, F32).astype(BF16)
    return re, im


def _fft_stage1(load_slab, f1_ref, a_scr, n1, n2):
    def body(s, carry):
        a = jnp.dot(f1_ref[...], load_slab(s), preferred_element_type=F32)
        a_scr[pl.ds(pl.multiple_of(s * _pitch(n1), SUBLANES), n1), :] = _pack_complex(a[:n1], a[n1:])
        return carry
    lax.fori_loop(0, n2, body, 0, unroll=FFT_UNROLL)


def _pitch(n1):
    return n1 + SUBLANES


def _load_freq_rows(a_scr, k1, n1, n2):
    re, im = _unpack_complex(a_scr[pl.ds(k1, n2, stride=_pitch(n1)), :])
    return jnp.concatenate([re, im], axis=0)


def _shortconv_kernel(p_ref, w_ref, b_ref, o_ref, u_scr, *, seq, n2, chunk):
    n1h = seq // n2
    up = _pitch(n2)
    w0, w1, w2 = w_ref[0:1, :], w_ref[1:2, :], w_ref[2:3, :]
    bias = b_ref[...]
    halo = 16
    for c in range(seq // chunk):
        r0 = c * chunk
        lo = max(r0 - halo, 0)
        hi = min(r0 + chunk + halo, seq)
        win = p_ref[0, lo:hi, :].astype(F32)
        off = r0 - lo
        cur = win[off:off + chunk]
        if r0 > 0:
            prev = win[off - 1:off - 1 + chunk]
        else:
            prev = jnp.concatenate([jnp.zeros((1, LANES), F32), win[0:chunk - 1]], axis=0)
        if r0 + chunk < seq:
            nxt = win[off + 1:off + 1 + chunk]
        else:
            nxt = jnp.concatenate([win[off + 1:off + chunk], jnp.zeros((1, LANES), F32)], axis=0)
        u = prev * w0 + cur * w1 + nxt * w2 + bias
        for gi in range(chunk // n2):
            r1 = (r0 // n2 + gi) * up
            u_scr[r1:r1 + n2, :] = u[gi * n2:(gi + 1) * n2]

    def body(s, carry):
        o_ref[0, s] = u_scr[pl.ds(s, n1h, stride=up), :].astype(o_ref.dtype)
        return carry
    lax.fori_loop(0, n2, body, 0, unroll=FFT_UNROLL)


def _shortconv(proj3, w_short, b_short, col0):
    b, seq, _ = proj3.shape
    cu = w_short.shape[1]
    _, n1, n2 = _fft_plan(seq)
    n1h = n1 // 2
    chunk = min(512, seq)
    cb0 = col0 // LANES
    return pl.pallas_call(
        functools.partial(_shortconv_kernel, seq=seq, n2=n2, chunk=chunk),
        grid=(b, cu // LANES),
        in_specs=[
            pl.BlockSpec((1, seq, LANES), lambda bi, c: (bi, 0, cb0 + c)),
            pl.BlockSpec((3, LANES), lambda bi, c: (0, c)),
            pl.BlockSpec((1, LANES), lambda bi, c: (0, c)),
        ],
        out_specs=pl.BlockSpec((1, n2, n1h, LANES), lambda bi, c: (bi, 0, 0, c)),
        out_shape=jax.ShapeDtypeStruct((b, n2, n1h, cu), BF16),
        scratch_shapes=[pltpu.VMEM((n1h * _pitch(n2), LANES), F32)],
        compiler_params=_cparams(("arbitrary", "arbitrary")),
        name="shortconv",
    )(proj3, w_short, b_short.reshape(1, cu))


def _filtergen_kernel(z_ref, w1_ref, b1_ref, f1_ref, w2_ref, b2_ref, f2_ref, w3_ref, b3_ref, f3_ref,
                      w4_ref, dl_ref, o_ref, h_scr):
    j = pl.program_id(1)

    @pl.when(j == 0)
    def _():
        z = z_ref[...]
        h = jnp.sin(f1_ref[...] * (_dot3(z, w1_ref[...]) + b1_ref[...]))
        h = jnp.sin(f2_ref[...] * (_dot3(h, w2_ref[...]) + b2_ref[...]))
        h_scr[...] = jnp.sin(f3_ref[...] * (_dot3(h, w3_ref[...]) + b3_ref[...]))

    t01 = z_ref[:, 0:1]
    decay = jnp.exp(-t01 * dl_ref[...])
    o_ref[...] = _dot3(h_scr[...], w4_ref[...]) * decay


@functools.lru_cache(maxsize=None)
def _filter_features(seq):
    _, n1, n2 = _fft_plan(seq)
    n1h = n1 // 2
    t01 = np.linspace(0.0, 1.0, seq, dtype=np.float32)[:, None]
    ang = (2.0 * math.pi * np.arange(seq, dtype=np.float32)[:, None] / seq).astype(np.float32)
    bands = np.linspace(1e-4, POS_BANDS - 1, POS_BANDS, dtype=np.float32)[None, :]
    arg = (bands * ang).astype(np.float32)
    z = np.concatenate([t01, np.cos(arg), -np.sin(arg)], axis=-1).astype(np.float32)
    pos = (np.arange(n1h)[None, :] * n2 + np.arange(n2)[:, None]).reshape(-1)
    zp = np.zeros((seq, FEAT_PAD), np.float32)
    zp[:, :z.shape[1]] = z[pos]
    return zp


def _filtergen(seq, w1, b1, f1, w2, b2, f2, w3, b3, f3, w4, hy_ch):
    fw = w1.shape[1]
    fo = w4.shape[1]
    z = jnp.asarray(_filter_features(seq))
    pe = z.shape[1]
    w1 = jnp.pad(w1, ((0, pe - w1.shape[0]), (0, 0)))
    deltas = np.abs(np.linspace(math.log(DECAY_FAST) / DECAY_TARGET, math.log(DECAY_SLOW) / DECAY_TARGET,
                                hy_ch, dtype=np.float32))
    dl = jnp.asarray(np.tile(deltas, fo // hy_ch).reshape(1, fo))
    tr = min(512, seq)
    tc = min(512, fo)
    row = lambda a: a.reshape(1, -1)
    full = lambda shp: pl.BlockSpec(shp, lambda i, j: (0, 0))
    return pl.pallas_call(
        _filtergen_kernel,
        grid=(seq // tr, fo // tc),
        in_specs=[
            pl.BlockSpec((tr, pe), lambda i, j: (i, 0)),
            full((pe, fw)), full((1, fw)), full((1, fw)),
            full((fw, fw)), full((1, fw)), full((1, fw)),
            full((fw, fw)), full((1, fw)), full((1, fw)),
            pl.BlockSpec((fw, tc), lambda i, j: (0, j)),
            pl.BlockSpec((1, tc), lambda i, j: (0, j)),
        ],
        out_specs=pl.BlockSpec((tr, tc), lambda i, j: (i, j)),
        out_shape=jax.ShapeDtypeStruct((seq, fo), F32),
        scratch_shapes=[pltpu.VMEM((tr, fw), F32)],
        compiler_params=_cparams(("arbitrary", "arbitrary")),
        name="filtergen",
    )(z, w1, row(b1), row(f1), w2, row(b2), row(f2), w3, row(b3), row(f3), w4, dl)


def _filterfft_kernel(hf_ref, hb_ref, f1_ref, g2_ref, o_ref, af_scr, ab_scr, *, n1, n2, k1c, scale):
    kc = pl.program_id(1)

    @pl.when(kc == 0)
    def _():
        _fft_stage1(lambda s: hf_ref[s].astype(BF16), f1_ref, af_scr, n1, n2)
        _fft_stage1(lambda s: hb_ref[s].astype(BF16), f1_ref, ab_scr, n1, n2)

    def body(kk, carry):
        k1 = kc * k1c + kk
        rows = jnp.concatenate([_load_freq_rows(af_scr, k1, n1, n2), _load_freq_rows(ab_scr, k1, n1, n2)], axis=1)
        x = jnp.dot(g2_ref[kk], rows, preferred_element_type=F32)
        xf, xb = x[:, :LANES], x[:, LANES:]
        o_ref[kk] = jnp.concatenate([xf[:n2] + xb[:n2], xf[n2:] - xb[n2:]], axis=0) * scale
        return carry
    lax.fori_loop(0, k1c, body, 0, unroll=min(k1c, STAGE2_UNROLL))


def _filter_spectrum(hs, seq, hy_ch):
    n, n1, n2 = _fft_plan(seq)
    n1h = n1 // 2
    cols = hs.shape[-1]
    cpt = hy_ch // LANES
    f1, _, g2, _ = _dft_constants(seq)
    f1r = jnp.asarray(f1[:, :n1h], BF16)
    g2b = jnp.asarray(g2, BF16)
    k1c = _k1_chunk(n1)
    hs3 = hs.reshape(n2, n1h, cols)
    scratch = pltpu.VMEM((n2 * _pitch(n1), LANES), jnp.uint32)
    return pl.pallas_call(
        functools.partial(_filterfft_kernel, n1=n1, n2=n2, k1c=k1c, scale=1.0 / n),
        grid=(2 * cpt, n1 // k1c),
        in_specs=[
            pl.BlockSpec((n2, n1h, LANES), lambda c, k: (0, 0, (c // cpt) * 2 * cpt + c % cpt)),
            pl.BlockSpec((n2, n1h, LANES), lambda c, k: (0, 0, (c // cpt) * 2 * cpt + cpt + c % cpt)),
            pl.BlockSpec((2 * n1, n1h), lambda c, k: (0, 0)),
            pl.BlockSpec((k1c, 2 * n2, 2 * n2), lambda c, k: (k, 0, 0)),
        ],
        out_specs=pl.BlockSpec((k1c, 2 * n2, LANES), lambda c, k: (k, 0, c)),
        out_shape=jax.ShapeDtypeStruct((n1, 2 * n2, 2 * hy_ch), F32),
        scratch_shapes=[scratch, scratch],
        compiler_params=_cparams(("arbitrary", "arbitrary")),
        name="filterfft",
    )(hs3, hs3, f1r, g2b)


def _k1_chunk(n1):
    return n1 if n1 <= 64 else 8


def _hyena_kernel(u_ref, g_ref, skip_ref, h_ref, f1_ref, f1i_ref, g2_ref, g2i_ref, o_ref, a_scr,
                  *z_scr, n1, n2, k1c):
    kc = pl.program_id(2)
    n1h = n1 // 2
    zp = _pitch(n1h)

    @pl.when(kc == 0)
    def _():
        _fft_stage1(lambda s: jnp.concatenate([u_ref[0, s], u_ref[1, s]], axis=0), f1_ref, a_scr, n1, n2)

    def body(kk, carry):
        k1 = kc * k1c + kk
        x = jnp.dot(g2_ref[kk], _load_freq_rows(a_scr, k1, n1, n2), preferred_element_type=F32)
        xr, xi = x[:n2], x[n2:]
        h = h_ref[kk]
        hr, hi = h[:n2], h[n2:]
        y = jnp.concatenate([xr * hr - xi * hi, xr * hi + xi * hr], axis=0).astype(BF16)
        bq = jnp.dot(g2i_ref[kk], y, preferred_element_type=F32)
        a_scr[pl.ds(k1, n2, stride=_pitch(n1)), :] = _pack_complex(bq[:n2], bq[n2:])
        return carry
    lax.fori_loop(0, k1c, body, 0, unroll=min(k1c, STAGE2_UNROLL))

    @pl.when(kc == pl.num_programs(2) - 1)
    def _():
        skip = skip_ref[...]

        def body3(s, carry):
            re, im = _unpack_complex(a_scr[pl.ds(pl.multiple_of(s * _pitch(n1), SUBLANES), n1), :])
            y = jnp.dot(f1i_ref[...], jnp.concatenate([re, im], axis=0), preferred_element_type=F32)
            for bb in range(2):
                yb = y[bb * n1h:(bb + 1) * n1h]
                z = g_ref[bb, s].astype(F32) * (yb + skip * u_ref[bb, s].astype(F32))
                if z_scr:
                    z_scr[0][bb, pl.ds(pl.multiple_of(s * zp, SUBLANES), n1h), :] = z
                else:
                    o_ref[bb, s] = z.astype(o_ref.dtype)
            return carry
        lax.fori_loop(0, n2, body3, 0, unroll=FFT_UNROLL)

        if z_scr:
            def body4(r, carry):
                for bb in range(2):
                    rows = z_scr[0][bb, pl.ds(r, n2, stride=zp), :]
                    o_ref[bb, pl.ds(pl.multiple_of(r * n2, n2), n2), :] = rows.astype(o_ref.dtype)
                return carry
            lax.fori_loop(0, n1h, body4, 0, unroll=FFT_UNROLL)


def _hyena_conv(u_arr, u_col, g_arr, g_col, skip, tf, order, seq, hy_ch, token_order):
    b = u_arr.shape[0]
    n, n1, n2 = _fft_plan(seq)
    n1h = n1 // 2
    f1, f1i, g2, g2i = _dft_constants(seq)
    k1c = _k1_chunk(n1)
    ucb, gcb = u_col // LANES, g_col // LANES
    hcb = order * hy_ch // LANES
    scratch = [pltpu.VMEM((n2 * _pitch(n1), LANES), jnp.uint32)]
    if token_order:
        out_spec = pl.BlockSpec((2, seq, LANES), lambda c, p, k: (p, 0, c))
        out_shape = jax.ShapeDtypeStruct((b, seq, hy_ch), BF16)
        scratch.append(pltpu.VMEM((2, n2 * _pitch(n1h), LANES), F32))
    else:
        out_spec = pl.BlockSpec((2, n2, n1h, LANES), lambda c, p, k: (p, 0, 0, c))
        out_shape = jax.ShapeDtypeStruct((b, n2, n1h, hy_ch), BF16)
    return pl.pallas_call(
        functools.partial(_hyena_kernel, n1=n1, n2=n2, k1c=k1c),
        grid=(hy_ch // LANES, b // 2, n1 // k1c),
        in_specs=[
            pl.BlockSpec((2, n2, n1h, LANES), lambda c, p, k: (p, 0, 0, ucb + c)),
            pl.BlockSpec((2, n2, n1h, LANES), lambda c, p, k: (p, 0, 0, gcb + c)),
            pl.BlockSpec((1, LANES), lambda c, p, k: (0, c)),
            pl.BlockSpec((k1c, 2 * n2, LANES), lambda c, p, k: (k, 0, hcb + c)),
            pl.BlockSpec((2 * n1, n1), lambda c, p, k: (0, 0)),
            pl.BlockSpec((n1, 2 * n1), lambda c, p, k: (0, 0)),
            pl.BlockSpec((k1c, 2 * n2, 2 * n2), lambda c, p, k: (k, 0, 0)),
            pl.BlockSpec((k1c, 2 * n2, 2 * n2), lambda c, p, k: (k, 0, 0)),
        ],
        out_specs=out_spec,
        out_shape=out_shape,
        scratch_shapes=scratch,
        compiler_params=_cparams(("arbitrary", "arbitrary", "arbitrary")),
        name="hyenaconv",
    )(u_arr, g_arr, skip.reshape(1, hy_ch), tf,
      jnp.asarray(f1, BF16), jnp.asarray(f1i, BF16), jnp.asarray(g2, BF16), jnp.asarray(g2i, BF16))


def _outproj_kernel(a_ref, z_ref, x_ref, wa_ref, wh_ref, gh_ref, gf_ref, wrh_ref, wrl_ref, br_ref, c0_ref,
                    tri_ref, x1_ref, xn_ref, idx_ref, gate_ref, rank_ref, cnt_ref, run_scr, *, n_exp):
    i = pl.program_id(0)

    @pl.when(i == 0)
    def _():
        run_scr[...] = c0_ref[...]

    z = z_ref[...].astype(F32)
    zn = (z * lax.rsqrt(jnp.mean(z * z, axis=-1, keepdims=True) + EPS) * gh_ref[...]).astype(BF16)
    mixed = (jnp.dot(a_ref[...], wa_ref[...], preferred_element_type=F32)
             + jnp.dot(zn, wh_ref[...], preferred_element_type=F32))
    x1 = x_ref[...] + mixed
    x1_ref[...] = x1
    xn = x1 * lax.rsqrt(jnp.mean(x1 * x1, axis=-1, keepdims=True) + EPS) * gf_ref[...]
    half = xn.shape[1] // 2
    packed = _pack_complex(xn[:, :half], xn[:, half:])
    for c in range(SUBLANES):
        xn_ref[pl.ds(c, xn.shape[0], stride=SUBLANES), :] = packed[:, c * LANES:(c + 1) * LANES]

    xh, xl = _split_bf16(xn)
    dg = functools.partial(lax.dot_general, dimension_numbers=NT_DIMS, preferred_element_type=F32)
    logit = dg(wrh_ref[...], xh) + dg(wrh_ref[...], xl) + dg(wrl_ref[...], xh) + br_ref[...]
    tm = logit.shape[1]
    eio = lax.broadcasted_iota(jnp.int32, (n_exp, tm), 0)
    vals, onehots = [], []
    cur = logit
    for k in range(TOP_K):
        m = jnp.max(cur, axis=0, keepdims=True)
        idx = jnp.min(jnp.where(cur == m, eio, n_exp), axis=0, keepdims=True)
        oh = eio == idx
        vals.append(m)
        onehots.append(oh)
        idx_ref[k:k + 1, :] = idx
        cur = jnp.where(oh, -jnp.inf, cur)
    ex = [jnp.exp(v - vals[0]) for v in vals]
    tot = ex[0] + ex[1] + ex[2] + ex[3]
    for k in range(TOP_K):
        gate_ref[k:k + 1, :] = ex[k] / tot
    ohsum = jnp.zeros((n_exp, tm), F32)
    for oh in onehots:
        ohsum = ohsum + oh.astype(F32)
    base = jnp.dot(ohsum.astype(BF16), tri_ref[...], preferred_element_type=F32) + run_scr[...]
    for k in range(TOP_K):
        rank_ref[k:k + 1, :] = jnp.sum(jnp.where(onehots[k], base, 0.0), axis=0, keepdims=True).astype(jnp.int32)
    run_scr[...] = run_scr[...] + jnp.sum(ohsum, axis=1, keepdims=True)
    cnt_ref[...] = run_scr[...]


def _outproj_router(attn_n, z2, x2d, wa, wh, g_hy, g_ffn, wr_hi, wr_lo, b_router, count0):
    t, d = x2d.shape
    aw, hc = attn_n.shape[1], z2.shape[1]
    n_exp = b_router.shape[0]
    tm = min(512, t)
    tri = jnp.asarray(np.triu(np.ones((tm, tm), np.float32), k=1), BF16)
    full = lambda shp: pl.BlockSpec(shp, lambda i: (0, 0))
    resident = lambda shp: pl.BlockSpec(shp, lambda i: (0, 0), pipeline_mode=pl.Buffered(1))
    rowblk = lambda w: pl.BlockSpec((tm, w), lambda i: (i, 0))
    colblk = pl.BlockSpec((TOP_K, tm), lambda i: (0, i))
    return pl.pallas_call(
        functools.partial(_outproj_kernel, n_exp=n_exp),
        grid=(t // tm,),
        in_specs=[rowblk(aw), rowblk(hc), rowblk(d), resident((aw, d)), resident((hc, d)), full((1, hc)),
                  full((1, d)), full((n_exp, d)), full((n_exp, d)), full((n_exp, 1)), full((n_exp, 1)),
                  resident((tm, tm))],
        out_specs=[rowblk(d), pl.BlockSpec((tm * SUBLANES, LANES), lambda i: (i, 0)),
                   colblk, colblk, colblk, full((n_exp, 1))],
        out_shape=[jax.ShapeDtypeStruct((t, d), F32), jax.ShapeDtypeStruct((t * SUBLANES, LANES), jnp.uint32),
                   jax.ShapeDtypeStruct((TOP_K, t), jnp.int32), jax.ShapeDtypeStruct((TOP_K, t), F32),
                   jax.ShapeDtypeStruct((TOP_K, t), jnp.int32), jax.ShapeDtypeStruct((n_exp, 1), F32)],
        scratch_shapes=[pltpu.VMEM((n_exp, 1), F32)],
        compiler_params=_cparams(("arbitrary",)),
        name="outproj_router",
    )(attn_n, z2, x2d, wa, wh, g_hy.reshape(1, hc), g_ffn.reshape(1, d), wr_hi, wr_lo,
      b_router.reshape(n_exp, 1), count0, tri)


def _dest_kernel(ps_ref, idx_ref, rank_ref, o_ref, *, n_exp):
    idx = idx_ref[...]
    acc = rank_ref[...]
    for e in range(n_exp):
        acc = acc + jnp.where(idx == e, ps_ref[e], 0)
    o_ref[...] = acc


def _dest_rows(pstart, idx, rank):
    k, t = idx.shape
    tb = next(c for c in (2048, 1024, 512, 256, 128) if t % c == 0)
    blk = pl.BlockSpec((k, tb), lambda i: (0, i))
    return pl.pallas_call(
        functools.partial(_dest_kernel, n_exp=pstart.shape[0]),
        grid=(t // tb,),
        in_specs=[pl.BlockSpec(memory_space=pltpu.SMEM), blk, blk],
        out_specs=blk,
        out_shape=jax.ShapeDtypeStruct((k, t), jnp.int32),
        compiler_params=_cparams(("arbitrary",)),
        name="dest",
    )(pstart, idx, rank)


def _zero_unrouted_rows(ps_ref, cnt_ref, xs_ref, zero_scr, sem, bulk_sem):
    zero_scr[...] = jnp.zeros(zero_scr.shape, zero_scr.dtype)
    n_exp = ps_ref.shape[0]
    bulk = zero_scr.shape[0]

    def bulk_copy(row):
        return pltpu.make_async_copy(zero_scr, xs_ref.at[pl.ds(row, bulk)], bulk_sem)

    def row_copy(row):
        return pltpu.make_async_copy(zero_scr.at[0], xs_ref.at[row], sem)

    def fill(first, count):
        n_bulk = count // bulk
        rest0 = first + n_bulk * bulk
        n_rest = count - n_bulk * bulk

        def issue_bulk(r, c):
            bulk_copy(first + r * bulk).start()
            return c
        lax.fori_loop(0, n_bulk, issue_bulk, 0)

        def issue_row(r, c):
            row_copy(rest0 + r).start()
            return c
        lax.fori_loop(0, n_rest, issue_row, 0)

        def drain_bulk(r, c):
            bulk_copy(0).wait()
            return c
        lax.fori_loop(0, n_bulk, drain_bulk, 0)

        def drain_row(r, c):
            row_copy(0).wait()
            return c
        lax.fori_loop(0, n_rest, drain_row, 0)

    def pad_rows(e):
        return lax.rem(MOE_ROWS - lax.rem(cnt_ref[e], MOE_ROWS), MOE_ROWS)

    def per_expert(e, carry):
        fill(ps_ref[e] + cnt_ref[e], pad_rows(e))
        return carry
    lax.fori_loop(0, n_exp, per_expert, 0)
    end = ps_ref[n_exp - 1] + cnt_ref[n_exp - 1] + pad_rows(n_exp - 1)
    fill(end, xs_ref.shape[0] - end)


def _dispatch_kernel(ps_ref, cnt_ref, dest_ref, xa_ref, xb_ref, xs_ref, sems, zero_scr, *, tiles_a):
    i = pl.program_id(0)
    tm = xa_ref.shape[0] // SUBLANES
    sem = sems.at[0]

    @pl.when(i == 0)
    def _():
        _zero_unrouted_rows(ps_ref, cnt_ref, xs_ref, zero_scr, sem, sems.at[1])

    def scatter_tile(xn_ref):
        def row_copy(t, d):
            src = xn_ref.at[pl.ds(pl.multiple_of(t * SUBLANES, SUBLANES), SUBLANES)]
            return pltpu.make_async_copy(src, xs_ref.at[d], sem)

        def issue(t, carry):
            for k in range(TOP_K):
                row_copy(t, dest_ref[k, t]).start(priority=k % 2)
            return carry
        lax.fori_loop(0, tm, issue, 0, unroll=4)

        def drain(t, carry):
            for k in range(TOP_K):
                row_copy(0, 0).wait()
            return carry
        lax.fori_loop(0, tm, drain, 0)

    @pl.when(i < tiles_a)
    def _():
        scatter_tile(xa_ref)

    @pl.when(i >= tiles_a)
    def _():
        scatter_tile(xb_ref)


def _dispatch(pstart, cnt, dest, xn_a, xn_b, n_rows):
    t = dest.shape[1]
    ta = xn_a.shape[0] // SUBLANES
    tm = min(256, ta, t - ta)
    assert ta % tm == 0 and (t - ta) % tm == 0
    tiles_a = ta // tm
    return pl.pallas_call(
        functools.partial(_dispatch_kernel, tiles_a=tiles_a),
        grid=(t // tm,),
        in_specs=[pl.BlockSpec(memory_space=pltpu.SMEM), pl.BlockSpec(memory_space=pltpu.SMEM),
                  pl.BlockSpec((TOP_K, tm), lambda i: (0, i), memory_space=pltpu.SMEM),
                  pl.BlockSpec((tm * SUBLANES, LANES), lambda i: (jnp.minimum(i, tiles_a - 1), 0)),
                  pl.BlockSpec((tm * SUBLANES, LANES), lambda i: (jnp.maximum(i - tiles_a, 0), 0))],
        out_specs=pl.BlockSpec(memory_space=pl.ANY),
        out_shape=jax.ShapeDtypeStruct((n_rows, SUBLANES, LANES), jnp.uint32),
        scratch_shapes=[pltpu.SemaphoreType.DMA((2,)), pltpu.VMEM((ZERO_ROWS, SUBLANES, LANES), jnp.uint32)],
        compiler_params=_cparams(("arbitrary",)),
        name="dispatch",
    )(pstart, cnt, dest, xn_a, xn_b)


def _expert_kernel(be_ref, nu_ref, x_ref, wgu_ref, bgu_ref, wd_ref, bd_ref, y_ref, x_scr, *, tf):
    del be_ref
    bm, d = x_scr.shape
    half = d // 2
    f = wd_ref.shape[1]

    @pl.when(pl.program_id(0) < nu_ref[0])
    def _():
        for c in range(SUBLANES):
            hi, lo = _unpack_complex(x_ref[pl.ds(c, bm, stride=SUBLANES), :])
            x_scr[:, c * LANES:(c + 1) * LANES] = hi
            x_scr[:, half + c * LANES:half + (c + 1) * LANES] = lo
        x = x_scr[...]
        acc = bd_ref[0]
        for c0 in range(0, f, tf):
            g = jnp.dot(x, wgu_ref[0, :, c0:c0 + tf], preferred_element_type=F32) + bgu_ref[0, :, c0:c0 + tf]
            u = (jnp.dot(x, wgu_ref[0, :, f + c0:f + c0 + tf], preferred_element_type=F32)
                 + bgu_ref[0, :, f + c0:f + c0 + tf])
            glu = jnp.minimum(g, SWIGLU_LIMIT)
            lin = jnp.clip(u, -SWIGLU_LIMIT, SWIGLU_LIMIT)
            h = glu * jax.nn.sigmoid(SWIGLU_ALPHA * glu) * (lin + 1.0)
            acc = acc + jnp.dot(h.astype(BF16), wd_ref[0, c0:c0 + tf, :], preferred_element_type=F32)
        for c in range(SUBLANES):
            y_ref[pl.ds(c, bm, stride=SUBLANES), :] = _pack_complex(
                acc[:, c * LANES:(c + 1) * LANES], acc[:, half + c * LANES:half + (c + 1) * LANES])


def _experts(block_expert, n_used, xs, w_gu, b_gu, w_dn, b_dn):
    n_exp, d, f2 = w_gu.shape
    n_rows = xs.shape[0] // SUBLANES
    f = f2 // 2
    tf = next(c for c in (512, 256, 128) if f % c == 0)
    bm = MOE_ROWS
    last = lambda i, nu: jnp.minimum(i, nu[0] - 1)
    resident = pl.Buffered(1)
    grid_spec = pltpu.PrefetchScalarGridSpec(
        num_scalar_prefetch=2,
        grid=(n_rows // bm,),
        in_specs=[
            pl.BlockSpec((bm * SUBLANES, LANES), lambda i, be, nu: (last(i, nu), 0)),
            pl.BlockSpec((1, d, f2), lambda i, be, nu: (be[i], 0, 0), pipeline_mode=resident),
            pl.BlockSpec((1, 1, f2), lambda i, be, nu: (be[i], 0, 0)),
            pl.BlockSpec((1, f, d), lambda i, be, nu: (be[i], 0, 0), pipeline_mode=resident),
            pl.BlockSpec((1, 1, d), lambda i, be, nu: (be[i], 0, 0)),
        ],
        out_specs=pl.BlockSpec((bm * SUBLANES, LANES), lambda i, be, nu: (last(i, nu), 0)),
        scratch_shapes=[pltpu.VMEM((bm, d), BF16)],
    )
    return pl.pallas_call(
        functools.partial(_expert_kernel, tf=tf),
        grid_spec=grid_spec,
        out_shape=jax.ShapeDtypeStruct((n_rows * SUBLANES, LANES), jnp.uint32),
        compiler_params=_cparams(("arbitrary",)),
        name="experts",
    )(block_expert, n_used, xs, w_gu, b_gu.reshape(n_exp, 1, f2), w_dn, b_dn.reshape(n_exp, 1, d))


def _combine_kernel(dest_ref, dest_next_ref, x1_ref, gate_ref, ys_ref, o_ref, buf, sems):
    i = pl.program_id(0)
    tm, d = x1_ref.shape
    half = d // 2
    slot = lax.rem(i, 2)

    def row_copy(s, t, k, d_row):
        dst = buf.at[s, k, pl.ds(pl.multiple_of(t * SUBLANES, SUBLANES), SUBLANES)]
        return pltpu.make_async_copy(ys_ref.at[d_row], dst, sems.at[s])

    def issue_tile(d_ref, s):
        def issue(t, carry):
            for k in range(TOP_K):
                row_copy(s, t, k, d_ref[k, t]).start(priority=k % 2)
            return carry
        lax.fori_loop(0, tm, issue, 0, unroll=4)

    @pl.when(i == 0)
    def _():
        issue_tile(dest_ref, 0)

    @pl.when(i + 1 < pl.num_programs(0))
    def _():
        issue_tile(dest_next_ref, 1 - slot)

    def drain(t, carry):
        for k in range(TOP_K):
            row_copy(slot, 0, 0, 0).wait()
        return carry
    lax.fori_loop(0, tm, drain, 0)

    gate = gate_ref[...]
    for c in range(SUBLANES):
        lo_cols = slice(c * LANES, (c + 1) * LANES)
        hi_cols = slice(half + c * LANES, half + (c + 1) * LANES)
        moe_a = x1_ref[:, lo_cols]
        moe_b = x1_ref[:, hi_cols]
        for k in range(TOP_K):
            w = buf[slot, k, pl.ds(c, tm, stride=SUBLANES), :]
            g = gate[:, k:k + 1]
            moe_a = moe_a + g * lax.bitcast_convert_type(w & jnp.uint32(0xFFFF0000), F32)
            moe_b = moe_b + g * lax.bitcast_convert_type(w << 16, F32)
        o_ref[:, lo_cols] = moe_a
        o_ref[:, hi_cols] = moe_b


def _combine(dest, x1, gate_tk, ys):
    t, d = x1.shape
    tm = min(256, t)
    nt = t // tm
    return pl.pallas_call(
        _combine_kernel,
        grid=(nt,),
        in_specs=[pl.BlockSpec((TOP_K, tm), lambda i: (0, i), memory_space=pltpu.SMEM),
                  pl.BlockSpec((TOP_K, tm), lambda i: (0, jnp.minimum(i + 1, nt - 1)), memory_space=pltpu.SMEM),
                  pl.BlockSpec((tm, d), lambda i: (i, 0)),
                  pl.BlockSpec((tm, TOP_K), lambda i: (i, 0)),
                  pl.BlockSpec(memory_space=pl.ANY)],
        out_specs=pl.BlockSpec((tm, d), lambda i: (i, 0)),
        out_shape=jax.ShapeDtypeStruct((t, d), F32),
        scratch_shapes=[pltpu.VMEM((2, TOP_K, tm * SUBLANES, LANES), jnp.uint32), pltpu.SemaphoreType.DMA((2,))],
        compiler_params=_cparams(("arbitrary",)),
        name="combine",
    )(dest, dest, x1, gate_tk, ys)


def _mixer(x3, g_mix, w_in_bf, g_q, g_k, sinks, w_short, b_short, flt, hyena_skip, g_attn_out, dims):
    n_heads, kv_w, hy_ch = dims
    b, seq, d = x3.shape
    attn_w = n_heads * HEAD_DIM
    proj = _inproj(x3.reshape(b * seq, d), g_mix, w_in_bf, g_q, g_k, attn_w, kv_w)
    proj3 = proj.reshape(b, seq, proj.shape[1])
    attn_n = _attention(proj3, sinks, g_attn_out, n_heads, kv_w)
    u = _shortconv(proj3, w_short, b_short, attn_w + 2 * kv_w)
    hs = _filtergen(seq, *flt, hy_ch)
    tf = _filter_spectrum(hs, seq, hy_ch)
    z1 = _hyena_conv(u, 0, u, hy_ch, hyena_skip[0], tf, 0, seq, hy_ch, token_order=False)
    z2 = _hyena_conv(z1, 0, u, 2 * hy_ch, hyena_skip[1], tf, 1, seq, hy_ch, token_order=True)
    return attn_n.reshape(b * seq, attn_w), z2.reshape(b * seq, hy_ch)


def kernel(x_prompt, x_sample, g_mix, w_in, g_q, g_k, attn_sinks, w_short, b_short, flt_w1, flt_b1, flt_freq1, flt_w2, flt_b2, flt_freq2, flt_w3, flt_b3, flt_freq3, flt_w4, hyena_skip, g_attn_out, g_hyena_out, w_out, g_ffn, w_router, b_router, w_gate_up, b_gate_up, w_down, b_down):
    assert g_mix.shape[0] == 1, "single-layer trunk"
    d = x_prompt.shape[-1]
    assert d == 2 * SUBLANES * LANES, "routed-row tile layout is built for d_model = 2048"
    n_heads = attn_sinks.shape[-1]
    hy_ch = hyena_skip.shape[-1]
    attn_w = n_heads * HEAD_DIM
    kv_w = (w_in.shape[-1] - attn_w - 3 * hy_ch) // 2
    n_exp = w_router.shape[-1]
    dims = (n_heads, kv_w, hy_ch)

    w_in_bf = w_in[0].astype(BF16)
    wa = w_out[0, :attn_w].astype(BF16)
    wh = w_out[0, attn_w:].astype(BF16)
    wr_t = w_router[0].T
    wr_hi = wr_t.astype(BF16)
    wr_lo = (wr_t - wr_hi.astype(F32)).astype(BF16)
    w_gu = w_gate_up[0].astype(BF16)
    w_dn = w_down[0].astype(BF16)
    flt = (flt_w1[0], flt_b1[0], flt_freq1[0], flt_w2[0], flt_b2[0], flt_freq2[0],
           flt_w3[0], flt_b3[0], flt_freq3[0], flt_w4[0])

    xs_in = (x_prompt, x_sample)
    routed = []
    counts = jnp.zeros((n_exp, 1), F32)
    for x3 in xs_in:
        attn_n, z2 = _mixer(x3, g_mix[0], w_in_bf, g_q[0], g_k[0], attn_sinks[0], w_short[0], b_short[0],
                            flt, hyena_skip[0], g_attn_out[0], dims)
        x1, xn, idx, gate, rank, counts = _outproj_router(
            attn_n, z2, x3.reshape(-1, d), wa, wh, g_hyena_out[0], g_ffn[0], wr_hi, wr_lo, b_router[0], counts)
        routed.append((x1, xn, idx, gate, rank))

    n_slots = sum(r[0].shape[0] for r in routed) * TOP_K
    n_rows = (n_slots // MOE_ROWS + n_exp) * MOE_ROWS
    cnt = counts[:, 0].astype(jnp.int32)
    padded = (cnt + MOE_ROWS - 1) // MOE_ROWS * MOE_ROWS
    pend = jnp.cumsum(padded)
    pstart = (pend - padded).astype(jnp.int32)
    n_blocks = n_rows // MOE_ROWS
    block_row0 = jnp.arange(n_blocks, dtype=jnp.int32) * MOE_ROWS
    block_expert = jnp.minimum(jnp.sum(pend[None, :] <= block_row0[:, None], axis=1), n_exp - 1).astype(jnp.int32)
    n_used = (pend[-1:] // MOE_ROWS).astype(jnp.int32)

    dest_all = _dest_rows(pstart, jnp.concatenate([r[2] for r in routed], axis=1),
                          jnp.concatenate([r[4] for r in routed], axis=1))
    t_a = routed[0][0].shape[0]
    dests = [dest_all[:, :t_a], dest_all[:, t_a:]]
    xs = _dispatch(pstart, cnt, dest_all, routed[0][1], routed[1][1], n_rows)
    ys = _experts(block_expert, n_used, xs.reshape(n_rows * SUBLANES, LANES), w_gu, b_gate_up[0], w_dn, b_down[0])
    ys = ys.reshape(n_rows, SUBLANES, LANES)
    outs = []
    for (x1, xn, idx, gate, rank), dest, x3 in zip(routed, dests, xs_in):
        outs.append(_combine(dest, x1, gate.T, ys).reshape(x3.shape))
    return tuple(outs)
```

```python
import functools
import math

import numpy as np
import jax
import jax.numpy as jnp
from jax import lax
from jax.experimental import pallas as pl
from jax.experimental.pallas import tpu as pltpu

HEAD_DIM = 64
WINDOW = 128
TOP_K = 4
EPS = 1e-6
POS_BANDS = 16
FEAT_PAD = 64
DECAY_FAST = 0.3
DECAY_SLOW = 1.5
DECAY_TARGET = 1e-2
SWIGLU_ALPHA = 1.702
SWIGLU_LIMIT = 7.0

LANES = 128
SUBLANES = 8
VMEM_LIMIT = 56 * 1024 * 1024
MOE_ROWS = 512
ZERO_ROWS = 64
FFT_UNROLL = 8
STAGE2_UNROLL = 8
F32 = jnp.float32
BF16 = jnp.bfloat16
NT_DIMS = (((1,), (1,)), ((), ()))


def _cparams(sem):
    return pltpu.CompilerParams(dimension_semantics=sem, vmem_limit_bytes=VMEM_LIMIT)


def _split_bf16(a):
    hi = a.astype(BF16)
    lo = (a - hi.astype(F32)).astype(BF16)
    return hi, lo


def _dot3(a, b):
    ah, al = _split_bf16(a)
    bh, bl = _split_bf16(b)
    d = functools.partial(jnp.dot, preferred_element_type=F32)
    return d(ah, bh) + d(ah, bl) + d(al, bh)


def _inproj_kernel(x_ref, g_ref, w_ref, gq_ref, gk_ref, s_ref, o_ref, *, n_q_tiles, tn):
    x = x_ref[...]
    ms = jnp.mean(x * x, axis=-1, keepdims=True)
    h = (x * lax.rsqrt(ms + EPS) * g_ref[...]).astype(BF16)

    def head_norm(a, g):
        hi, lo = _split_bf16(a * a)
        s = s_ref[:a.shape[1], :a.shape[1]]
        ss = (jnp.dot(hi, s, preferred_element_type=F32)
              + jnp.dot(lo, s, preferred_element_type=F32))
        return a * lax.rsqrt(ss * (1.0 / HEAD_DIM) + EPS) * g

    for j in range(w_ref.shape[1] // tn):
        cols = slice(j * tn, (j + 1) * tn)
        acc = jnp.dot(h, w_ref[:, cols], preferred_element_type=F32)
        if j < n_q_tiles:
            o_ref[:, cols] = head_norm(acc, gq_ref[...]).astype(o_ref.dtype)
        elif j == n_q_tiles:
            half = tn // 2
            o_ref[:, j * tn:j * tn + half] = head_norm(acc[:, :half], gk_ref[...]).astype(o_ref.dtype)
            o_ref[:, j * tn + half:(j + 1) * tn] = acc[:, half:].astype(o_ref.dtype)
        else:
            o_ref[:, cols] = acc.astype(o_ref.dtype)


def _inproj(x2d, g_mix, w_in_bf, g_q, g_k, attn_w, kv_w):
    t, d = x2d.shape
    in_w = w_in_bf.shape[1]
    tn = 2 * kv_w
    tm = min(512, t)
    assert attn_w % tn == 0 and in_w % tn == 0 and t % tm == 0
    n_q_tiles = attn_w // tn
    heads_per_tile = tn // HEAD_DIM
    blk = np.kron(np.eye(heads_per_tile, dtype=np.float32), np.ones((HEAD_DIM, HEAD_DIM), np.float32))
    gq_t = jnp.tile(g_q.reshape(1, HEAD_DIM), (1, heads_per_tile))
    gk_t = jnp.tile(g_k.reshape(1, HEAD_DIM), (1, heads_per_tile // 2))
    return pl.pallas_call(
        functools.partial(_inproj_kernel, n_q_tiles=n_q_tiles, tn=tn),
        grid=(t // tm,),
        in_specs=[
            pl.BlockSpec((tm, d), lambda i: (i, 0)),
            pl.BlockSpec((1, d), lambda i: (0, 0)),
            pl.BlockSpec((d, in_w), lambda i: (0, 0), pipeline_mode=pl.Buffered(1)),
            pl.BlockSpec((1, tn), lambda i: (0, 0)),
            pl.BlockSpec((1, tn // 2), lambda i: (0, 0)),
            pl.BlockSpec((tn, tn), lambda i: (0, 0)),
        ],
        out_specs=pl.BlockSpec((tm, in_w), lambda i: (i, 0)),
        out_shape=jax.ShapeDtypeStruct((t, in_w), BF16),
        compiler_params=_cparams(("arbitrary",)),
        name="inproj",
    )(x2d, g_mix.reshape(1, d), w_in_bf, gq_t, gk_t, jnp.asarray(blk, BF16))


def _kv_start(i, tq, span, seq):
    return jnp.clip(i * tq - WINDOW, 0, seq - span)


def _attn_kernel(q_ref, k_ref, v_ref, bias_ref, sink_ref, g_ref, o_ref, o_scr, s_scr, p_scr,
                 *, n_heads, n_kv, tq, span, seq):
    i = pl.program_id(1)
    grp = n_heads // n_kv
    rows = 2 * SUBLANES
    start = pl.multiple_of(_kv_start(i, tq, span, seq), WINDOW)
    for kh in range(n_kv):
        kk = k_ref[0, pl.ds(start, span), kh * HEAD_DIM:(kh + 1) * HEAD_DIM] * (HEAD_DIM ** -0.5)
        for h in range(kh * grp, (kh + 1) * grp):
            q = q_ref[0, :, h * HEAD_DIM:(h + 1) * HEAD_DIM]
            s_scr[h * tq:(h + 1) * tq, :] = (lax.dot_general(q, kk, NT_DIMS, preferred_element_type=F32)
                                             + bias_ref[0, h * tq:(h + 1) * tq, :])
    for h in range(n_heads):
        sink = sink_ref[h]
        for r in range(h * tq, (h + 1) * tq, rows):
            s = s_scr[r:r + rows, :]
            m = jnp.maximum(jnp.max(s, axis=-1, keepdims=True), sink)
            p = jnp.exp(s - m)
            denom = jnp.sum(p, axis=-1, keepdims=True) + jnp.exp(sink - m)
            p_scr[r:r + rows, :] = (p / denom).astype(BF16)
    for kh in range(n_kv):
        vv = v_ref[0, pl.ds(start, span), kh * HEAD_DIM:(kh + 1) * HEAD_DIM]
        for h in range(kh * grp, (kh + 1) * grp):
            o_scr[:, h * HEAD_DIM:(h + 1) * HEAD_DIM] = jnp.dot(p_scr[h * tq:(h + 1) * tq, :], vv,
                                                                preferred_element_type=F32)
    a = o_scr[...]
    ms = jnp.mean(a * a, axis=-1, keepdims=True)
    o_ref[0] = (a * lax.rsqrt(ms + EPS) * g_ref[...]).astype(o_ref.dtype)


@functools.lru_cache(maxsize=None)
def _alibi_window_bias(n_heads, tq, span):
    slopes = np.exp2(-8.0 * np.arange(1, n_heads + 1, dtype=np.float32) / n_heads).astype(np.float32)
    r = np.arange(tq)[:, None]
    c = np.arange(span)[None, :]
    tables = []
    for v in range(3):
        dist = np.abs(r - c + v * WINDOW)
        bias = np.where(dist <= WINDOW, -slopes[:, None, None] * dist.astype(np.float32), -np.inf)
        tables.append(bias.reshape(n_heads * tq, span))
    return np.stack(tables).astype(np.float32)


def _attention(proj3, sinks, g_attn_out, n_heads, kv_w):
    b, seq, _ = proj3.shape
    attn_w = n_heads * HEAD_DIM
    n_kv = kv_w // HEAD_DIM
    tq = WINDOW
    span = min(3 * WINDOW, seq)
    kcol = attn_w // kv_w
    bias = jnp.asarray(_alibi_window_bias(n_heads, tq, span))
    return pl.pallas_call(
        functools.partial(_attn_kernel, n_heads=n_heads, n_kv=n_kv, tq=tq, span=span, seq=seq),
        grid=(b, seq // tq),
        in_specs=[
            pl.BlockSpec((1, tq, attn_w), lambda bi, i: (bi, i, 0)),
            pl.BlockSpec((1, seq, kv_w), lambda bi, i: (bi, 0, kcol)),
            pl.BlockSpec((1, seq, kv_w), lambda bi, i: (bi, 0, kcol + 1)),
            pl.BlockSpec((1, n_heads * tq, span),
                         lambda bi, i: ((i * tq - _kv_start(i, tq, span, seq)) // WINDOW, 0, 0)),
            pl.BlockSpec(memory_space=pltpu.SMEM),
            pl.BlockSpec((1, attn_w), lambda bi, i: (0, 0)),
        ],
        out_specs=pl.BlockSpec((1, tq, attn_w), lambda bi, i: (bi, i, 0)),
        out_shape=jax.ShapeDtypeStruct((b, seq, attn_w), BF16),
        scratch_shapes=[pltpu.VMEM((tq, attn_w), F32), pltpu.VMEM((n_heads * tq, span), F32),
                        pltpu.VMEM((n_heads * tq, span), BF16)],
        compiler_params=_cparams(("arbitrary", "arbitrary")),
        name="attention",
    )(proj3, proj3, proj3, bias, sinks.astype(F32), g_attn_out.reshape(1, attn_w))


def _fft_plan(seq):
    n = 2 * seq
    n2 = 1 << (int(math.log2(n)) // 2)
    n1 = n // n2
    return n, n1, n2


@functools.lru_cache(maxsize=None)
def _dft_constants(seq):
    n, n1, n2 = _fft_plan(seq)
    n1h = n1 // 2

    def blockc(re, im):
        return np.block([[re, -im], [im, re]])

    k1 = np.arange(n1)[:, None]
    a1 = 2.0 * np.pi * (k1 * np.arange(n1h)[None, :]) / n1
    f1 = blockc(np.cos(a1), -np.sin(a1))
    f1i = blockc(np.cos(a1.T), np.sin(a1.T))
    kk1 = np.arange(n1)[:, None, None]
    kk2 = np.arange(n2)[None, :, None]
    nn2 = np.arange(n2)[None, None, :]
    a2 = 2.0 * np.pi * (nn2 * (kk1 + n1 * kk2)) / n
    c2, s2 = np.cos(a2), np.sin(a2)
    g2 = np.concatenate([np.concatenate([c2, s2], axis=2),
                         np.concatenate([-s2, c2], axis=2)], axis=1)
    c2t, s2t = np.swapaxes(c2, 1, 2), np.swapaxes(s2, 1, 2)
    g2i = np.concatenate([np.concatenate([c2t, -s2t], axis=2),
                          np.concatenate([s2t, c2t], axis=2)], axis=1)
    return (f1.astype(np.float32), f1i.astype(np.float32),
            g2.astype(np.float32), g2i.astype(np.float32))


def _pack_complex(re, im):
    rb = lax.bitcast_convert_type(re.astype(BF16).astype(F32), jnp.uint32)
    ib = lax.bitcast_convert_type(im.astype(BF16).astype(F32), jnp.uint32)
    return (rb & jnp.uint32(0xFFFF0000)) | (ib >> 16)


def _unpack_complex(w):
    re = lax.bitcast_convert_type(w & jnp.uint32(0xFFFF0000), F32).astype(BF16)
    im = lax.bitcast_convert_type(w << 16, F32).astype(BF16)
    return re, im


def _fft_stage1(load_slab, f1_ref, a_scr, n1, n2):
    def body(s, carry):
        a = jnp.dot(f1_ref[...], load_slab(s), preferred_element_type=F32)
        a_scr[pl.ds(pl.multiple_of(s * _pitch(n1), SUBLANES), n1), :] = _pack_complex(a[:n1], a[n1:])
        return carry
    lax.fori_loop(0, n2, body, 0, unroll=FFT_UNROLL)


def _pitch(n1):
    return n1 + SUBLANES


def _load_freq_rows(a_scr, k1, n1, n2):
    re, im = _unpack_complex(a_scr[pl.ds(k1, n2, stride=_pitch(n1)), :])
    return jnp.concatenate([re, im], axis=0)


def _shortconv_kernel(p_ref, w_ref, b_ref, o_ref, u_scr, *, seq, n2, chunk):
    n1h = seq // n2
    up = _pitch(n2)
    w0, w1, w2 = w_ref[0:1, :], w_ref[1:2, :], w_ref[2:3, :]
    bias = b_ref[...]
    halo = 16
    for c in range(seq // chunk):
        r0 = c * chunk
        lo = max(r0 - halo, 0)
        hi = min(r0 + chunk + halo, seq)
        win = p_ref[0, lo:hi, :].astype(F32)
        off = r0 - lo
        cur = win[off:off + chunk]
        if r0 > 0:
            prev = win[off - 1:off - 1 + chunk]
        else:
            prev = jnp.concatenate([jnp.zeros((1, LANES), F32), win[0:chunk - 1]], axis=0)
        if r0 + chunk < seq:
            nxt = win[off + 1:off + 1 + chunk]
        else:
            nxt = jnp.concatenate([win[off + 1:off + chunk], jnp.zeros((1, LANES), F32)], axis=0)
        u = prev * w0 + cur * w1 + nxt * w2 + bias
        for gi in range(chunk // n2):
            r1 = (r0 // n2 + gi) * up
            u_scr[r1:r1 + n2, :] = u[gi * n2:(gi + 1) * n2]

    def body(s, carry):
        o_ref[0, s] = u_scr[pl.ds(s, n1h, stride=up), :].astype(o_ref.dtype)
        return carry
    lax.fori_loop(0, n2, body, 0, unroll=FFT_UNROLL)


def _shortconv(proj3, w_short, b_short, col0):
    b, seq, _ = proj3.shape
    cu = w_short.shape[1]
    _, n1, n2 = _fft_plan(seq)
    n1h = n1 // 2
    chunk = min(512, seq)
    cb0 = col0 // LANES
    return pl.pallas_call(
        functools.partial(_shortconv_kernel, seq=seq, n2=n2, chunk=chunk),
        grid=(b, cu // LANES),
        in_specs=[
            pl.BlockSpec((1, seq, LANES), lambda bi, c: (bi, 0, cb0 + c)),
            pl.BlockSpec((3, LANES), lambda bi, c: (0, c)),
            pl.BlockSpec((1, LANES), lambda bi, c: (0, c)),
        ],
        out_specs=pl.BlockSpec((1, n2, n1h, LANES), lambda bi, c: (bi, 0, 0, c)),
        out_shape=jax.ShapeDtypeStruct((b, n2, n1h, cu), BF16),
        scratch_shapes=[pltpu.VMEM((n1h * _pitch(n2), LANES), F32)],
        compiler_params=_cparams(("arbitrary", "arbitrary")),
        name="shortconv",
    )(proj3, w_short, b_short.reshape(1, cu))


def _filtergen_kernel(z_ref, w1_ref, b1_ref, f1_ref, w2_ref, b2_ref, f2_ref, w3_ref, b3_ref, f3_ref,
                      w4_ref, dl_ref, o_ref, h_scr):
    j = pl.program_id(1)

    @pl.when(j == 0)
    def _():
        z = z_ref[...]
        h = jnp.sin(f1_ref[...] * (_dot3(z, w1_ref[...]) + b1_ref[...]))
        h = jnp.sin(f2_ref[...] * (_dot3(h, w2_ref[...]) + b2_ref[...]))
        h_scr[...] = jnp.sin(f3_ref[...] * (_dot3(h, w3_ref[...]) + b3_ref[...]))

    t01 = z_ref[:, 0:1]
    decay = jnp.exp(-t01 * dl_ref[...])
    o_ref[...] = _dot3(h_scr[...], w4_ref[...]) * decay


@functools.lru_cache(maxsize=None)
def _filter_features(seq):
    _, n1, n2 = _fft_plan(seq)
    n1h = n1 // 2
    t01 = np.linspace(0.0, 1.0, seq, dtype=np.float32)[:, None]
    ang = (2.0 * math.pi * np.arange(seq, dtype=np.float32)[:, None] / seq).astype(np.float32)
    bands = np.linspace(1e-4, POS_BANDS - 1, POS_BANDS, dtype=np.float32)[None, :]
    arg = (bands * ang).astype(np.float32)
    z = np.concatenate([t01, np.cos(arg), -np.sin(arg)], axis=-1).astype(np.float32)
    pos = (np.arange(n1h)[None, :] * n2 + np.arange(n2)[:, None]).reshape(-1)
    zp = np.zeros((seq, FEAT_PAD), np.float32)
    zp[:, :z.shape[1]] = z[pos]
    return zp


def _filtergen(seq, w1, b1, f1, w2, b2, f2, w3, b3, f3, w4, hy_ch):
    fw = w1.shape[1]
    fo = w4.shape[1]
    z = jnp.asarray(_filter_features(seq))
    pe = z.shape[1]
    w1 = jnp.pad(w1, ((0, pe - w1.shape[0]), (0, 0)))
    deltas = np.abs(np.linspace(math.log(DECAY_FAST) / DECAY_TARGET, math.log(DECAY_SLOW) / DECAY_TARGET,
                                hy_ch, dtype=np.float32))
    dl = jnp.asarray(np.tile(deltas, fo // hy_ch).reshape(1, fo))
    tr = min(512, seq)
    tc = min(512, fo)
    row = lambda a: a.reshape(1, -1)
    full = lambda shp: pl.BlockSpec(shp, lambda i, j: (0, 0))
    return pl.pallas_call(
        _filtergen_kernel,
        grid=(seq // tr, fo // tc),
        in_specs=[
            pl.BlockSpec((tr, pe), lambda i, j: (i, 0)),
            full((pe, fw)), full((1, fw)), full((1, fw)),
            full((fw, fw)), full((1, fw)), full((1, fw)),
            full((fw, fw)), full((1, fw)), full((1, fw)),
            pl.BlockSpec((fw, tc), lambda i, j: (0, j)),
            pl.BlockSpec((1, tc), lambda i, j: (0, j)),
        ],
        out_specs=pl.BlockSpec((tr, tc), lambda i, j: (i, j)),
        out_shape=jax.ShapeDtypeStruct((seq, fo), F32),
        scratch_shapes=[pltpu.VMEM((tr, fw), F32)],
        compiler_params=_cparams(("arbitrary", "arbitrary")),
        name="filtergen",
    )(z, w1, row(b1), row(f1), w2, row(b2), row(f2), w3, row(b3), row(f3), w4, dl)


def _filterfft_kernel(hf_ref, hb_ref, f1_ref, g2_ref, o_ref, af_scr, ab_scr, *, n1, n2, k1c, scale):
    kc = pl.program_id(1)

    @pl.when(kc == 0)
    def _():
        _fft_stage1(lambda s: hf_ref[s].astype(BF16), f1_ref, af_scr, n1, n2)
        _fft_stage1(lambda s: hb_ref[s].astype(BF16), f1_ref, ab_scr, n1, n2)

    def body(kk, carry):
        k1 = kc * k1c + kk
        rows = jnp.concatenate([_load_freq_rows(af_scr, k1, n1, n2), _load_freq_rows(ab_scr, k1, n1, n2)], axis=1)
        x = jnp.dot(g2_ref[kk], rows, preferred_element_type=F32)
        xf, xb = x[:, :LANES], x[:, LANES:]
        o_ref[kk] = jnp.concatenate([xf[:n2] + xb[:n2], xf[n2:] - xb[n2:]], axis=0) * scale
        return carry
    lax.fori_loop(0, k1c, body, 0, unroll=min(k1c, STAGE2_UNROLL))


def _filter_spectrum(hs, seq, hy_ch):
    n, n1, n2 = _fft_plan(seq)
    n1h = n1 // 2
    cols = hs.shape[-1]
    cpt = hy_ch // LANES
    f1, _, g2, _ = _dft_constants(seq)
    f1r = jnp.asarray(f1[:, :n1h], BF16)
    g2b = jnp.asarray(g2, BF16)
    k1c = _k1_chunk(n1)
    hs3 = hs.reshape(n2, n1h, cols)
    scratch = pltpu.VMEM((n2 * _pitch(n1), LANES), jnp.uint32)
    return pl.pallas_call(
        functools.partial(_filterfft_kernel, n1=n1, n2=n2, k1c=k1c, scale=1.0 / n),
        grid=(2 * cpt, n1 // k1c),
        in_specs=[
            pl.BlockSpec((n2, n1h, LANES), lambda c, k: (0, 0, (c // cpt) * 2 * cpt + c % cpt)),
            pl.BlockSpec((n2, n1h, LANES), lambda c, k: (0, 0, (c // cpt) * 2 * cpt + cpt + c % cpt)),
            pl.BlockSpec((2 * n1, n1h), lambda c, k: (0, 0)),
            pl.BlockSpec((k1c, 2 * n2, 2 * n2), lambda c, k: (k, 0, 0)),
        ],
        out_specs=pl.BlockSpec((k1c, 2 * n2, LANES), lambda c, k: (k, 0, c)),
        out_shape=jax.ShapeDtypeStruct((n1, 2 * n2, 2 * hy_ch), F32),
        scratch_shapes=[scratch, scratch],
        compiler_params=_cparams(("arbitrary", "arbitrary")),
        name="filterfft",
    )(hs3, hs3, f1r, g2b)


def _k1_chunk(n1):
    return n1 if n1 <= 64 else 8


def _hyena_kernel(u_ref, g_ref, skip_ref, h_ref, f1_ref, f1i_ref, g2_ref, g2i_ref, o_ref, a_scr,
                  *z_scr, n1, n2, k1c):
    kc = pl.program_id(2)
    n1h = n1 // 2
    zp = _pitch(n1h)

    @pl.when(kc == 0)
    def _():
        _fft_stage1(lambda s: jnp.concatenate([u_ref[0, s], u_ref[1, s]], axis=0), f1_ref, a_scr, n1, n2)

    def body(kk, carry):
        k1 = kc * k1c + kk
        x = jnp.dot(g2_ref[kk], _load_freq_rows(a_scr, k1, n1, n2), preferred_element_type=F32)
        xr, xi = x[:n2], x[n2:]
        h = h_ref[kk]
        hr, hi = h[:n2], h[n2:]
        y = jnp.concatenate([xr * hr - xi * hi, xr * hi + xi * hr], axis=0).astype(BF16)
        bq = jnp.dot(g2i_ref[kk], y, preferred_element_type=F32)
        a_scr[pl.ds(k1, n2, stride=_pitch(n1)), :] = _pack_complex(bq[:n2], bq[n2:])
        return carry
    lax.fori_loop(0, k1c, body, 0, unroll=min(k1c, STAGE2_UNROLL))

    @pl.when(kc == pl.num_programs(2) - 1)
    def _():
        skip = skip_ref[...]

        def body3(s, carry):
            re, im = _unpack_complex(a_scr[pl.ds(pl.multiple_of(s * _pitch(n1), SUBLANES), n1), :])
            y = jnp.dot(f1i_ref[...], jnp.concatenate([re, im], axis=0), preferred_element_type=F32)
            for bb in range(2):
                yb = y[bb * n1h:(bb + 1) * n1h]
                z = g_ref[bb, s].astype(F32) * (yb + skip * u_ref[bb, s].astype(F32))
                if z_scr:
                    z_scr[0][bb, pl.ds(pl.multiple_of(s * zp, SUBLANES), n1h), :] = z
                else:
                    o_ref[bb, s] = z.astype(o_ref.dtype)
            return carry
        lax.fori_loop(0, n2, body3, 0, unroll=FFT_UNROLL)

        if z_scr:
            def body4(r, carry):
                for bb in range(2):
                    rows = z_scr[0][bb, pl.ds(r, n2, stride=zp), :]
                    o_ref[bb, pl.ds(pl.multiple_of(r * n2, n2), n2), :] = rows.astype(o_ref.dtype)
                return carry
            lax.fori_loop(0, n1h, body4, 0, unroll=FFT_UNROLL)


def _hyena_conv(u_arr, u_col, g_arr, g_col, skip, tf, order, seq, hy_ch, token_order):
    b = u_arr.shape[0]
    n, n1, n2 = _fft_plan(seq)
    n1h = n1 // 2
    f1, f1i, g2, g2i = _dft_constants(seq)
    k1c = _k1_chunk(n1)
    ucb, gcb = u_col // LANES, g_col // LANES
    hcb = order * hy_ch // LANES
    scratch = [pltpu.VMEM((n2 * _pitch(n1), LANES), jnp.uint32)]
    if token_order:
        out_spec = pl.BlockSpec((2, seq, LANES), lambda c, p, k: (p, 0, c))
        out_shape = jax.ShapeDtypeStruct((b, seq, hy_ch), BF16)
        scratch.append(pltpu.VMEM((2, n2 * _pitch(n1h), LANES), F32))
    else:
        out_spec = pl.BlockSpec((2, n2, n1h, LANES), lambda c, p, k: (p, 0, 0, c))
        out_shape = jax.ShapeDtypeStruct((b, n2, n1h, hy_ch), BF16)
    return pl.pallas_call(
        functools.partial(_hyena_kernel, n1=n1, n2=n2, k1c=k1c),
        grid=(hy_ch // LANES, b // 2, n1 // k1c),
        in_specs=[
            pl.BlockSpec((2, n2, n1h, LANES), lambda c, p, k: (p, 0, 0, ucb + c)),
            pl.BlockSpec((2, n2, n1h, LANES), lambda c, p, k: (p, 0, 0, gcb + c)),
            pl.BlockSpec((1, LANES), lambda c, p, k: (0, c)),
            pl.BlockSpec((k1c, 2 * n2, LANES), lambda c, p, k: (k, 0, hcb + c)),
            pl.BlockSpec((2 * n1, n1), lambda c, p, k: (0, 0)),
            pl.BlockSpec((n1, 2 * n1), lambda c, p, k: (0, 0)),
            pl.BlockSpec((k1c, 2 * n2, 2 * n2), lambda c, p, k: (k, 0, 0)),
            pl.BlockSpec((k1c, 2 * n2, 2 * n2), lambda c, p, k: (k, 0, 0)),
        ],
        out_specs=out_spec,
        out_shape=out_shape,
        scratch_shapes=scratch,
        compiler_params=_cparams(("arbitrary", "arbitrary", "arbitrary")),
        name="hyenaconv",
    )(u_arr, g_arr, skip.reshape(1, hy_ch), tf,
      jnp.asarray(f1, BF16), jnp.asarray(f1i, BF16), jnp.asarray(g2, BF16), jnp.asarray(g2i, BF16))


def _outproj_kernel(a_ref, z_ref, x_ref, wa_ref, wh_ref, gh_ref, gf_ref, wrh_ref, wrl_ref, br_ref, c0_ref,
                    tri_ref, x1_ref, xn_ref, idx_ref, gate_ref, rank_ref, cnt_ref, run_scr, *, n_exp):
    i = pl.program_id(0)

    @pl.when(i == 0)
    def _():
        run_scr[...] = c0_ref[...]

    z = z_ref[...].astype(F32)
    zn = (z * lax.rsqrt(jnp.mean(z * z, axis=-1, keepdims=True) + EPS) * gh_ref[...]).astype(BF16)
    mixed = (jnp.dot(a_ref[...], wa_ref[...], preferred_element_type=F32)
             + jnp.dot(zn, wh_ref[...], preferred_element_type=F32))
    x1 = x_ref[...] + mixed
    x1_ref[...] = x1
    xn = x1 * lax.rsqrt(jnp.mean(x1 * x1, axis=-1, keepdims=True) + EPS) * gf_ref[...]
    half = xn.shape[1] // 2
    packed = _pack_complex(xn[:, :half], xn[:, half:])
    for c in range(SUBLANES):
        xn_ref[pl.ds(c, xn.shape[0], stride=SUBLANES), :] = packed[:, c * LANES:(c + 1) * LANES]

    xh, xl = _split_bf16(xn)
    dg = functools.partial(lax.dot_general, dimension_numbers=NT_DIMS, preferred_element_type=F32)
    logit = dg(wrh_ref[...], xh) + dg(wrh_ref[...], xl) + dg(wrl_ref[...], xh) + br_ref[...]
    tm = logit.shape[1]
    eio = lax.broadcasted_iota(jnp.int32, (n_exp, tm), 0)
    vals, onehots = [], []
    cur = logit
    for k in range(TOP_K):
        m = jnp.max(cur, axis=0, keepdims=True)
        idx = jnp.min(jnp.where(cur == m, eio, n_exp), axis=0, keepdims=True)
        oh = eio == idx
        vals.append(m)
        onehots.append(oh)
        idx_ref[k:k + 1, :] = idx
        cur = jnp.where(oh, -jnp.inf, cur)
    ex = [jnp.exp(v - vals[0]) for v in vals]
    tot = ex[0] + ex[1] + ex[2] + ex[3]
    for k in range(TOP_K):
        gate_ref[k:k + 1, :] = ex[k] / tot
    ohsum = jnp.zeros((n_exp, tm), F32)
    for oh in onehots:
        ohsum = ohsum + oh.astype(F32)
    base = jnp.dot(ohsum.astype(BF16), tri_ref[...], preferred_element_type=F32) + run_scr[...]
    for k in range(TOP_K):
        rank_ref[k:k + 1, :] = jnp.sum(jnp.where(onehots[k], base, 0.0), axis=0, keepdims=True).astype(jnp.int32)
    run_scr[...] = run_scr[...] + jnp.sum(ohsum, axis=1, keepdims=True)
    cnt_ref[...] = run_scr[...]


def _outproj_router(attn_n, z2, x2d, wa, wh, g_hy, g_ffn, wr_hi, wr_lo, b_router, count0):
    t, d = x2d.shape
    aw, hc = attn_n.shape[1], z2.shape[1]
    n_exp = b_router.shape[0]
    tm = min(512, t)
    tri = jnp.asarray(np.triu(np.ones((tm, tm), np.float32), k=1), BF16)
    full = lambda shp: pl.BlockSpec(shp, lambda i: (0, 0))
    resident = lambda shp: pl.BlockSpec(shp, lambda i: (0, 0), pipeline_mode=pl.Buffered(1))
    rowblk = lambda w: pl.BlockSpec((tm, w), lambda i: (i, 0))
    colblk = pl.BlockSpec((TOP_K, tm), lambda i: (0, i))
    return pl.pallas_call(
        functools.partial(_outproj_kernel, n_exp=n_exp),
        grid=(t // tm,),
        in_specs=[rowblk(aw), rowblk(hc), rowblk(d), resident((aw, d)), resident((hc, d)), full((1, hc)),
                  full((1, d)), full((n_exp, d)), full((n_exp, d)), full((n_exp, 1)), full((n_exp, 1)),
                  resident((tm, tm))],
        out_specs=[rowblk(d), pl.BlockSpec((tm * SUBLANES, LANES), lambda i: (i, 0)),
                   colblk, colblk, colblk, full((n_exp, 1))],
        out_shape=[jax.ShapeDtypeStruct((t, d), F32), jax.ShapeDtypeStruct((t * SUBLANES, LANES), jnp.uint32),
                   jax.ShapeDtypeStruct((TOP_K, t), jnp.int32), jax.ShapeDtypeStruct((TOP_K, t), F32),
                   jax.ShapeDtypeStruct((TOP_K, t), jnp.int32), jax.ShapeDtypeStruct((n_exp, 1), F32)],
        scratch_shapes=[pltpu.VMEM((n_exp, 1), F32)],
        compiler_params=_cparams(("arbitrary",)),
        name="outproj_router",
    )(attn_n, z2, x2d, wa, wh, g_hy.reshape(1, hc), g_ffn.reshape(1, d), wr_hi, wr_lo,
      b_router.reshape(n_exp, 1), count0, tri)


def _dest_kernel(ps_ref, idx_ref, rank_ref, o_ref, *, n_exp):
    idx = idx_ref[...]
    acc = rank_ref[...]
    for e in range(n_exp):
        acc = acc + jnp.where(idx == e, ps_ref[e], 0)
    o_ref[...] = acc


def _dest_rows(pstart, idx, rank):
    k, t = idx.shape
    tb = next(c for c in (2048, 1024, 512, 256, 128) if t % c == 0)
    blk = pl.BlockSpec((k, tb), lambda i: (0, i))
    return pl.pallas_call(
        functools.partial(_dest_kernel, n_exp=pstart.shape[0]),
        grid=(t // tb,),
        in_specs=[pl.BlockSpec(memory_space=pltpu.SMEM), blk, blk],
        out_specs=blk,
        out_shape=jax.ShapeDtypeStruct((k, t), jnp.int32),
        compiler_params=_cparams(("arbitrary",)),
        name="dest",
    )(pstart, idx, rank)


def _zero_unrouted_rows(ps_ref, cnt_ref, xs_ref, zero_scr, sem, bulk_sem):
    zero_scr[...] = jnp.zeros(zero_scr.shape, zero_scr.dtype)
    n_exp = ps_ref.shape[0]
    bulk = zero_scr.shape[0]

    def bulk_copy(row):
        return pltpu.make_async_copy(zero_scr, xs_ref.at[pl.ds(row, bulk)], bulk_sem)

    def row_copy(row):
        return pltpu.make_async_copy(zero_scr.at[0], xs_ref.at[row], sem)

    def fill(first, count):
        n_bulk = count // bulk
        rest0 = first + n_bulk * bulk
        n_rest = count - n_bulk * bulk

        def issue_bulk(r, c):
            bulk_copy(first + r * bulk).start()
            return c
        lax.fori_loop(0, n_bulk, issue_bulk, 0)

        def issue_row(r, c):
            row_copy(rest0 + r).start()
            return c
        lax.fori_loop(0, n_rest, issue_row, 0)

        def drain_bulk(r, c):
            bulk_copy(0).wait()
            return c
        lax.fori_loop(0, n_bulk, drain_bulk, 0)

        def drain_row(r, c):
            row_copy(0).wait()
            return c
        lax.fori_loop(0, n_rest, drain_row, 0)

    def pad_rows(e):
        return lax.rem(MOE_ROWS - lax.rem(cnt_ref[e], MOE_ROWS), MOE_ROWS)

    def per_expert(e, carry):
        fill(ps_ref[e] + cnt_ref[e], pad_rows(e))
        return carry
    lax.fori_loop(0, n_exp, per_expert, 0)
    end = ps_ref[n_exp - 1] + cnt_ref[n_exp - 1] + pad_rows(n_exp - 1)
    fill(end, xs_ref.shape[0] - end)


def _dispatch_kernel(ps_ref, cnt_ref, dest_ref, xa_ref, xb_ref, xs_ref, sems, zero_scr, *, tiles_a):
    i = pl.program_id(0)
    tm = xa_ref.shape[0] // SUBLANES
    sem = sems.at[0]

    @pl.when(i == 0)
    def _():
        _zero_unrouted_rows(ps_ref, cnt_ref, xs_ref, zero_scr, sem, sems.at[1])

    def scatter_tile(xn_ref):
        def row_copy(t, d):
            src = xn_ref.at[pl.ds(pl.multiple_of(t * SUBLANES, SUBLANES), SUBLANES)]
            return pltpu.make_async_copy(src, xs_ref.at[d], sem)

        def issue(t, carry):
            for k in range(TOP_K):
                row_copy(t, dest_ref[k, t]).start(priority=k % 2)
            return carry
        lax.fori_loop(0, tm, issue, 0, unroll=8)

        span = xs_ref.at[pl.ds(0, TOP_K * tm)]
        pltpu.make_async_copy(span, span, sem).wait()

    @pl.when(i < tiles_a)
    def _():
        scatter_tile(xa_ref)

    @pl.when(i >= tiles_a)
    def _():
        scatter_tile(xb_ref)


def _dispatch(pstart, cnt, dest, xn_a, xn_b, n_rows):
    t = dest.shape[1]
    ta = xn_a.shape[0] // SUBLANES
    tm = min(256, ta, t - ta)
    assert ta % tm == 0 and (t - ta) % tm == 0
    tiles_a = ta // tm
    return pl.pallas_call(
        functools.partial(_dispatch_kernel, tiles_a=tiles_a),
        grid=(t // tm,),
        in_specs=[pl.BlockSpec(memory_space=pltpu.SMEM), pl.BlockSpec(memory_space=pltpu.SMEM),
                  pl.BlockSpec((TOP_K, tm), lambda i: (0, i), memory_space=pltpu.SMEM),
                  pl.BlockSpec((tm * SUBLANES, LANES), lambda i: (jnp.minimum(i, tiles_a - 1), 0)),
                  pl.BlockSpec((tm * SUBLANES, LANES), lambda i: (jnp.maximum(i - tiles_a, 0), 0))],
        out_specs=pl.BlockSpec(memory_space=pl.ANY),
        out_shape=jax.ShapeDtypeStruct((n_rows, SUBLANES, LANES), jnp.uint32),
        scratch_shapes=[pltpu.SemaphoreType.DMA((2,)), pltpu.VMEM((ZERO_ROWS, SUBLANES, LANES), jnp.uint32)],
        compiler_params=_cparams(("arbitrary",)),
        name="dispatch",
    )(pstart, cnt, dest, xn_a, xn_b)


def _expert_kernel(be_ref, nu_ref, x_ref, wgu_ref, bgu_ref, wd_ref, bd_ref, y_ref, x_scr, *, tf):
    del be_ref
    bm, d = x_scr.shape
    half = d // 2
    f = wd_ref.shape[1]

    @pl.when(pl.program_id(0) >= nu_ref[0])
    def _():
        y_ref[...] = jnp.zeros(y_ref.shape, y_ref.dtype)

    @pl.when(pl.program_id(0) < nu_ref[0])
    def _():
        for c in range(SUBLANES):
            hi, lo = _unpack_complex(x_ref[pl.ds(c, bm, stride=SUBLANES), :])
            x_scr[:, c * LANES:(c + 1) * LANES] = hi
            x_scr[:, half + c * LANES:half + (c + 1) * LANES] = lo
        x = x_scr[...]
        acc = bd_ref[0]
        for c0 in range(0, f, tf):
            g = jnp.dot(x, wgu_ref[0, :, c0:c0 + tf], preferred_element_type=F32) + bgu_ref[0, :, c0:c0 + tf]
            u = (jnp.dot(x, wgu_ref[0, :, f + c0:f + c0 + tf], preferred_element_type=F32)
                 + bgu_ref[0, :, f + c0:f + c0 + tf])
            glu = jnp.minimum(g, SWIGLU_LIMIT)
            lin = jnp.clip(u, -SWIGLU_LIMIT, SWIGLU_LIMIT)
            h = glu * jax.nn.sigmoid(SWIGLU_ALPHA * glu) * (lin + 1.0)
            acc = acc + jnp.dot(h.astype(BF16), wd_ref[0, c0:c0 + tf, :], preferred_element_type=F32)
        for c in range(SUBLANES):
            y_ref[pl.ds(c, bm, stride=SUBLANES), :] = _pack_complex(
                acc[:, c * LANES:(c + 1) * LANES], acc[:, half + c * LANES:half + (c + 1) * LANES])


def _experts(block_expert, n_used, xs, w_gu, b_gu, w_dn, b_dn):
    n_exp, d, f2 = w_gu.shape
    n_rows = xs.shape[0] // SUBLANES
    f = f2 // 2
    tf = next(c for c in (512, 256, 128) if f % c == 0)
    bm = MOE_ROWS
    last = lambda i, nu: jnp.minimum(i, nu[0] - 1)
    resident = pl.Buffered(1)
    grid_spec = pltpu.PrefetchScalarGridSpec(
        num_scalar_prefetch=2,
        grid=(n_rows // bm,),
        in_specs=[
            pl.BlockSpec((bm * SUBLANES, LANES), lambda i, be, nu: (last(i, nu), 0)),
            pl.BlockSpec((1, d, f2), lambda i, be, nu: (be[i], 0, 0), pipeline_mode=resident),
            pl.BlockSpec((1, 1, f2), lambda i, be, nu: (be[i], 0, 0)),
            pl.BlockSpec((1, f, d), lambda i, be, nu: (be[i], 0, 0), pipeline_mode=resident),
            pl.BlockSpec((1, 1, d), lambda i, be, nu: (be[i], 0, 0)),
        ],
        out_specs=pl.BlockSpec((bm * SUBLANES, LANES), lambda i, be, nu: (i, 0)),
        scratch_shapes=[pltpu.VMEM((bm, d), BF16)],
    )
    return pl.pallas_call(
        functools.partial(_expert_kernel, tf=tf),
        grid_spec=grid_spec,
        out_shape=jax.ShapeDtypeStruct((n_rows * SUBLANES, LANES), jnp.uint32),
        compiler_params=_cparams(("arbitrary",)),
        name="experts",
    )(block_expert, n_used, xs, w_gu, b_gu.reshape(n_exp, 1, f2), w_dn, b_dn.reshape(n_exp, 1, d))


def _combine_kernel(dest_ref, dest_next_ref, x1_ref, gate_ref, ys_ref, o_ref, buf, sems):
    i = pl.program_id(0)
    tm, d = x1_ref.shape
    half = d // 2
    slot = lax.rem(i, 2)

    def row_copy(s, t, k, d_row):
        dst = buf.at[s, k, pl.ds(pl.multiple_of(t * SUBLANES, SUBLANES), SUBLANES)]
        return pltpu.make_async_copy(ys_ref.at[d_row], dst, sems.at[s])

    def issue_tile(d_ref, s):
        def issue(t, carry):
            for k in range(TOP_K):
                row_copy(s, t, k, d_ref[k, t]).start(priority=k % 2)
            return carry
        lax.fori_loop(0, tm, issue, 0, unroll=8)

    @pl.when(i == 0)
    def _():
        issue_tile(dest_ref, 0)

    @pl.when(i + 1 < pl.num_programs(0))
    def _():
        issue_tile(dest_next_ref, 1 - slot)

    pltpu.make_async_copy(buf.at[slot], buf.at[slot], sems.at[slot]).wait()

    gate = gate_ref[...]
    for c in range(SUBLANES):
        lo_cols = slice(c * LANES, (c + 1) * LANES)
        hi_cols = slice(half + c * LANES, half + (c + 1) * LANES)
        moe_a = x1_ref[:, lo_cols]
        moe_b = x1_ref[:, hi_cols]
        for k in range(TOP_K):
            w = buf[slot, k, pl.ds(c, tm, stride=SUBLANES), :]
            g = gate[:, k:k + 1]
            moe_a = moe_a + g * lax.bitcast_convert_type(w & jnp.uint32(0xFFFF0000), F32)
            moe_b = moe_b + g * lax.bitcast_convert_type(w << 16, F32)
        o_ref[:, lo_cols] = moe_a
        o_ref[:, hi_cols] = moe_b


def _combine(dest, x1, gate_tk, ys):
    t, d = x1.shape
    tm = min(256, t)
    nt = t // tm
    return pl.pallas_call(
        _combine_kernel,
        grid=(nt,),
        in_specs=[pl.BlockSpec((TOP_K, tm), lambda i: (0, i), memory_space=pltpu.SMEM),
                  pl.BlockSpec((TOP_K, tm), lambda i: (0, jnp.minimum(i + 1, nt - 1)), memory_space=pltpu.SMEM),
                  pl.BlockSpec((tm, d), lambda i: (i, 0)),
                  pl.BlockSpec((tm, TOP_K), lambda i: (i, 0)),
                  pl.BlockSpec(memory_space=pl.ANY)],
        out_specs=pl.BlockSpec((tm, d), lambda i: (i, 0)),
        out_shape=jax.ShapeDtypeStruct((t, d), F32),
        scratch_shapes=[pltpu.VMEM((2, TOP_K, tm * SUBLANES, LANES), jnp.uint32), pltpu.SemaphoreType.DMA((2,))],
        compiler_params=_cparams(("arbitrary",)),
        name="combine",
    )(dest, dest, x1, gate_tk, ys)


def _mixer(x3, g_mix, w_in_bf, g_q, g_k, sinks, w_short, b_short, flt, hyena_skip, g_attn_out, dims):
    n_heads, kv_w, hy_ch = dims
    b, seq, d = x3.shape
    attn_w = n_heads * HEAD_DIM
    proj = _inproj(x3.reshape(b * seq, d), g_mix, w_in_bf, g_q, g_k, attn_w, kv_w)
    proj3 = proj.reshape(b, seq, proj.shape[1])
    attn_n = _attention(proj3, sinks, g_attn_out, n_heads, kv_w)
    u = _shortconv(proj3, w_short, b_short, attn_w + 2 * kv_w)
    hs = _filtergen(seq, *flt, hy_ch)
    tf = _filter_spectrum(hs, seq, hy_ch)
    z1 = _hyena_conv(u, 0, u, hy_ch, hyena_skip[0], tf, 0, seq, hy_ch, token_order=False)
    z2 = _hyena_conv(z1, 0, u, 2 * hy_ch, hyena_skip[1], tf, 1, seq, hy_ch, token_order=True)
    return attn_n.reshape(b * seq, attn_w), z2.reshape(b * seq, hy_ch)


def kernel(x_prompt, x_sample, g_mix, w_in, g_q, g_k, attn_sinks, w_short, b_short, flt_w1, flt_b1, flt_freq1, flt_w2, flt_b2, flt_freq2, flt_w3, flt_b3, flt_freq3, flt_w4, hyena_skip, g_attn_out, g_hyena_out, w_out, g_ffn, w_router, b_router, w_gate_up, b_gate_up, w_down, b_down):
    assert g_mix.shape[0] == 1, "single-layer trunk"
    d = x_prompt.shape[-1]
    assert d == 2 * SUBLANES * LANES, "routed-row tile layout is built for d_model = 2048"
    n_heads = attn_sinks.shape[-1]
    hy_ch = hyena_skip.shape[-1]
    attn_w = n_heads * HEAD_DIM
    kv_w = (w_in.shape[-1] - attn_w - 3 * hy_ch) // 2
    n_exp = w_router.shape[-1]
    dims = (n_heads, kv_w, hy_ch)

    w_in_bf = w_in[0].astype(BF16)
    wa = w_out[0, :attn_w].astype(BF16)
    wh = w_out[0, attn_w:].astype(BF16)
    wr_t = w_router[0].T
    wr_hi = wr_t.astype(BF16)
    wr_lo = (wr_t - wr_hi.astype(F32)).astype(BF16)
    w_gu = w_gate_up[0].astype(BF16)
    w_dn = w_down[0].astype(BF16)
    flt = (flt_w1[0], flt_b1[0], flt_freq1[0], flt_w2[0], flt_b2[0], flt_freq2[0],
           flt_w3[0], flt_b3[0], flt_freq3[0], flt_w4[0])

    xs_in = (x_prompt, x_sample)
    routed = []
    counts = jnp.zeros((n_exp, 1), F32)
    for x3 in xs_in:
        attn_n, z2 = _mixer(x3, g_mix[0], w_in_bf, g_q[0], g_k[0], attn_sinks[0], w_short[0], b_short[0],
                            flt, hyena_skip[0], g_attn_out[0], dims)
        x1, xn, idx, gate, rank, counts = _outproj_router(
            attn_n, z2, x3.reshape(-1, d), wa, wh, g_hyena_out[0], g_ffn[0], wr_hi, wr_lo, b_router[0], counts)
        routed.append((x1, xn, idx, gate, rank))

    n_slots = sum(r[0].shape[0] for r in routed) * TOP_K
    n_rows = (n_slots // MOE_ROWS + n_exp) * MOE_ROWS
    cnt = counts[:, 0].astype(jnp.int32)
    padded = (cnt + MOE_ROWS - 1) // MOE_ROWS * MOE_ROWS
    pend = jnp.cumsum(padded)
    pstart = (pend - padded).astype(jnp.int32)
    n_blocks = n_rows // MOE_ROWS
    block_row0 = jnp.arange(n_blocks, dtype=jnp.int32) * MOE_ROWS
    block_expert = jnp.minimum(jnp.sum(pend[None, :] <= block_row0[:, None], axis=1), n_exp - 1).astype(jnp.int32)
    n_used = (pend[-1:] // MOE_ROWS).astype(jnp.int32)

    dest_all = _dest_rows(pstart, jnp.concatenate([r[2] for r in routed], axis=1),
                          jnp.concatenate([r[4] for r in routed], axis=1))
    t_a = routed[0][0].shape[0]
    dests = [dest_all[:, :t_a], dest_all[:, t_a:]]
    xs = _dispatch(pstart, cnt, dest_all, routed[0][1], routed[1][1], n_rows)
    ys = _experts(block_expert, n_used, xs.reshape(n_rows * SUBLANES, LANES), w_gu, b_gate_up[0], w_dn, b_down[0])
    ys = ys.reshape(n_rows, SUBLANES, LANES)
    outs = []
    for (x1, xn, idx, gate, rank), dest, x3 in zip(routed, dests, xs_in):
        outs.append(_combine(dest, x1, gate.T, ys).reshape(x3.shape))
    return tuple(outs)
```
